```python
import jax, jax.numpy as jnp
from jax import lax
import numpy as np

D_MODEL = 1024
BATCH = 8
SEQ = 2048
DEPTH = 1
DEC_BATCH = 128
DEC_SEQ = 8
PAST_LEN = 16384
PAGE_SIZE = 128

CONV_DIM = D_MODEL // 2
CONV_W = 3
GLA_HEADS = 4
GLA_DK = D_MODEL // 2
GLA_DV = D_MODEL
GLA_HEAD_K = GLA_DK // GLA_HEADS
GLA_HEAD_V = GLA_DV // GLA_HEADS
GLA_GATE_RANK = 16
GLA_GATE_TAU = 16.0
GLA_CHUNK = 16
PEER_HEADS = 8
PEER_NKEYS = 128
PEER_EXPERTS = PEER_NKEYS * PEER_NKEYS
PEER_QDIM = 256
PEER_HALF = PEER_QDIM // 2
PEER_TOPK = 16
PEER_BLOCK = 128
N_MOD = 6
EPS = 1e-6
IN_SIZES = (CONV_DIM, CONV_DIM, CONV_DIM, GLA_DK, GLA_DK, GLA_DV, GLA_DV, D_MODEL, D_MODEL, GLA_GATE_RANK)
W_IN_COLS = sum(IN_SIZES)

kernel_name = "hybrid_conv_gla_peer_decoder_step"


def rms_norm(x, w):
    x32 = x.astype(jnp.float32)
    y = x32 * lax.rsqrt(jnp.mean(x32 * x32, axis=-1, keepdims=True) + EPS)
    return (y * w.astype(jnp.float32)).astype(x.dtype)


def short_conv(u, buf, w, bias):
    t = u.shape[1]
    full = jnp.concatenate([buf.astype(u.dtype), u], axis=1)
    y = bias + w[CONV_W - 1] * full[:, CONV_W - 1:CONV_W - 1 + t]
    for i in range(CONV_W - 1):
        y = y + w[i] * full[:, i:i + t]
    return y, full[:, t:].astype(buf.dtype)


def gla_chunked(q, k, v, log_a, s0):
    b, t, h, _ = q.shape
    dv = v.shape[-1]
    n_chunks = -(-t // GLA_CHUNK)
    pad = n_chunks * GLA_CHUNK - t

    def prep(a):
        a = jnp.pad(a.astype(jnp.float32), ((0, 0), (0, pad), (0, 0), (0, 0)))
        return a.reshape(b, n_chunks, GLA_CHUNK, h, a.shape[-1]).transpose(1, 0, 3, 2, 4)

    qc, kc, vc, gc = prep(q), prep(k), prep(v), prep(log_a)
    causal = jnp.tril(jnp.ones((GLA_CHUNK, GLA_CHUNK), dtype=bool))[None, None, :, :, None]

    def step(s, inp):
        qi, ki, vi, gi = inp
        cum = jnp.cumsum(gi, axis=2)
        o_inter = jnp.einsum('bhcd,bhde->bhce', qi * jnp.exp(cum), s)
        rel = cum[:, :, :, None, :] - cum[:, :, None, :, :]
        decay = jnp.exp(jnp.where(causal, rel, -jnp.inf))
        scores = jnp.einsum('bhid,bhjd,bhijd->bhij', qi, ki, decay)
        o = o_inter + jnp.einsum('bhij,bhje->bhie', scores, vi)
        last = cum[:, :, -1:, :]
        s_new = jnp.exp(last[:, :, 0, :])[..., None] * s + jnp.einsum('bhjd,bhje->bhde', ki * jnp.exp(last - cum), vi)
        return s_new, o

    s_fin, o = lax.scan(step, s0.astype(jnp.float32), (qc, kc, vc, gc))
    o = o.transpose(1, 0, 3, 2, 4).reshape(b, n_chunks * GLA_CHUNK, h, dv)[:, :t]
    return o, s_fin


def mixer(h, conv_buf, gla_state, w_in, conv_w, conv_b, w_alpha, b_alpha, gla_norm_w, w_proj_a, w_proj_b, w_out):
    bsz, t, _ = h.shape
    points = [int(p) for p in np.cumsum(IN_SIZES)[:-1]]
    bg, cg, hin, q, k, v, r, ga, gb, alow = jnp.split(h @ w_in, points, axis=-1)
    y_conv, new_buf = short_conv(cg * hin, conv_buf, conv_w, conv_b)
    branch_a = (bg * y_conv) @ w_proj_a
    log_a = jax.nn.log_sigmoid((alow @ w_alpha + b_alpha).astype(jnp.float32)) / GLA_GATE_TAU
    qh = q.reshape(bsz, t, GLA_HEADS, GLA_HEAD_K) * (GLA_HEAD_K ** -0.5)
    kh = k.reshape(bsz, t, GLA_HEADS, GLA_HEAD_K)
    vh = v.reshape(bsz, t, GLA_HEADS, GLA_HEAD_V)
    ah = log_a.reshape(bsz, t, GLA_HEADS, GLA_HEAD_K)
    o, s_new = gla_chunked(qh, kh, vh, ah, gla_state)
    o = rms_norm(o.astype(h.dtype), gla_norm_w.reshape(GLA_HEADS, GLA_HEAD_V))
    o = o.reshape(bsz, t, GLA_DV) * jax.nn.silu(r)
    branch_b = o @ w_proj_b
    merged = jax.nn.sigmoid(ga) * branch_a + jax.nn.sigmoid(gb) * branch_b
    return merged @ w_out, new_buf, s_new.astype(gla_state.dtype)


def peer(h, wq, subkeys, u_tab, v_tab):
    b, t, d = h.shape
    n = b * t
    nb = -(-n // PEER_BLOCK)
    tokens = jnp.pad(h.reshape(n, d), ((0, nb * PEER_BLOCK - n), (0, 0))).reshape(nb, PEER_BLOCK, d)

    def block(xb):
        q = (xb @ wq).reshape(PEER_BLOCK, PEER_HEADS, 2, PEER_HALF)
        s = jnp.einsum('thpd,hpkd->thpk', q, subkeys)
        sv, si = lax.top_k(s, PEER_TOPK)
        cand = (sv[:, :, 0, :, None] + sv[:, :, 1, None, :]).reshape(PEER_BLOCK, PEER_HEADS, PEER_TOPK * PEER_TOPK)
        cand_id = (si[:, :, 0, :, None] * PEER_NKEYS + si[:, :, 1, None, :]).reshape(PEER_BLOCK, PEER_HEADS, PEER_TOPK * PEER_TOPK)
        top_v, top_i = lax.top_k(cand, PEER_TOPK)
        expert = jnp.take_along_axis(cand_id, top_i, axis=-1)
        gate = jax.nn.softmax(top_v.astype(jnp.float32), axis=-1).astype(xb.dtype)
        act = jax.nn.gelu(jnp.einsum('thkd,td->thk', u_tab[expert], xb))
        return jnp.einsum('thk,thkd->td', gate * act, v_tab[expert])

    out = lax.map(block, tokens)
    return out.reshape(nb * PEER_BLOCK, d)[:n].reshape(b, t, d)


def decoder_layer(x, c, conv_buf, gla_state, params):
    (n_mix_pre, n_mix_post, n_ffn_pre, n_ffn_post, w_ada, b_ada, w_in, conv_w, conv_b,
     w_alpha, b_alpha, gla_norm_w, w_proj_a, w_proj_b, w_out, peer_wq, peer_subkeys, peer_u, peer_v) = params
    mod = (jax.nn.silu(c) @ w_ada + b_ada)[:, None, :]
    sh1, sc1, g1, sh2, sc2, g2 = jnp.split(mod, N_MOD, axis=-1)
    h = rms_norm(x, n_mix_pre) * (1.0 + sc1) + sh1
    m, new_buf, new_state = mixer(h, conv_buf, gla_state, w_in, conv_w, conv_b, w_alpha, b_alpha,
                                  gla_norm_w, w_proj_a, w_proj_b, w_out)
    x = x + g1 * rms_norm(m, n_mix_post)
    h = rms_norm(x, n_ffn_pre) * (1.0 + sc2) + sh2
    x = x + g2 * rms_norm(peer(h, peer_wq, peer_subkeys, peer_u, peer_v), n_ffn_post)
    return x, new_buf, new_state


def setup_inputs(seed: int = 0) -> dict:
    key = jax.random.key(seed)
    ks = jax.random.split(key, 26)

    def nrm(k, shape, scale):
        return jax.random.normal(k, shape, jnp.float32) * scale

    def gain(k, width):
        return 1.0 + nrm(k, (DEPTH, width), 0.05)

    return {
        'x_prompt': nrm(ks[0], (BATCH, SEQ, D_MODEL), 1.0),
        'x_sample': nrm(ks[1], (DEC_BATCH, DEC_SEQ, D_MODEL), 1.0),
        'c_prompt': nrm(ks[2], (BATCH, D_MODEL), 1.0),
        'c_sample': nrm(ks[3], (DEC_BATCH, D_MODEL), 1.0),
        'state_conv': nrm(ks[4], (DEPTH, DEC_BATCH, CONV_W - 1, CONV_DIM), 1.0),
        'state_gla': nrm(ks[5], (DEPTH, DEC_BATCH, GLA_HEADS, GLA_HEAD_K, GLA_HEAD_V), 1.0),
        'norm_mix_pre': gain(ks[6], D_MODEL),
        'norm_mix_post': gain(ks[7], D_MODEL),
        'norm_ffn_pre': gain(ks[8], D_MODEL),
        'norm_ffn_post': gain(ks[9], D_MODEL),
        'w_ada': nrm(ks[10], (DEPTH, D_MODEL, N_MOD * D_MODEL), 0.5 * D_MODEL ** -0.5),
        'b_ada': nrm(ks[11], (DEPTH, N_MOD * D_MODEL), 0.02),
        'w_in': nrm(ks[12], (DEPTH, D_MODEL, W_IN_COLS), D_MODEL ** -0.5),
        'conv_w': nrm(ks[13], (DEPTH, CONV_W, CONV_DIM), CONV_W ** -0.5),
        'conv_b': nrm(ks[14], (DEPTH, CONV_DIM), 0.02),
        'w_alpha': nrm(ks[15], (DEPTH, GLA_GATE_RANK, GLA_DK), GLA_GATE_RANK ** -0.5),
        'b_alpha': nrm(ks[16], (DEPTH, GLA_DK), 0.1),
        'gla_norm_w': gain(ks[17], GLA_DV),
        'w_proj_a': nrm(ks[18], (DEPTH, CONV_DIM, D_MODEL), CONV_DIM ** -0.5),
        'w_proj_b': nrm(ks[19], (DEPTH, GLA_DV, D_MODEL), GLA_DV ** -0.5),
        'w_out': nrm(ks[20], (DEPTH, D_MODEL, D_MODEL), D_MODEL ** -0.5),
        'peer_wq': nrm(ks[21], (DEPTH, D_MODEL, PEER_HEADS * PEER_QDIM), D_MODEL ** -0.5),
        'peer_subkeys': nrm(ks[22], (DEPTH, PEER_HEADS, 2, PEER_NKEYS, PEER_HALF), PEER_HALF ** -0.5),
        'peer_u': nrm(ks[23], (DEPTH, PEER_EXPERTS, D_MODEL), D_MODEL ** -0.5),
        'peer_v': nrm(ks[24], (DEPTH, PEER_EXPERTS, D_MODEL), PEER_HEADS ** -0.5),
    }


def reference(x_prompt, x_sample, c_prompt, c_sample, state_conv, state_gla,
              norm_mix_pre, norm_mix_post, norm_ffn_pre, norm_ffn_post, w_ada, b_ada,
              w_in, conv_w, conv_b, w_alpha, b_alpha, gla_norm_w, w_proj_a, w_proj_b, w_out,
              peer_wq, peer_subkeys, peer_u, peer_v):
    yp, ys = x_prompt, x_sample
    conv_p, gla_p, conv_s, gla_s = [], [], [], []
    n_prompt = x_prompt.shape[0]
    for l in range(DEPTH):
        params = (norm_mix_pre[l], norm_mix_post[l], norm_ffn_pre[l], norm_ffn_post[l], w_ada[l], b_ada[l],
                  w_in[l], conv_w[l], conv_b[l], w_alpha[l], b_alpha[l], gla_norm_w[l],
                  w_proj_a[l], w_proj_b[l], w_out[l], peer_wq[l], peer_subkeys[l], peer_u[l], peer_v[l])
        buf0 = jnp.zeros((n_prompt, CONV_W - 1, CONV_DIM), state_conv.dtype)
        s0 = jnp.zeros((n_prompt, GLA_HEADS, GLA_HEAD_K, GLA_HEAD_V), state_gla.dtype)
        yp, bp, sp = decoder_layer(yp, c_prompt, buf0, s0, params)
        ys, bs, ss = decoder_layer(ys, c_sample, state_conv[l], state_gla[l], params)
        conv_p.append(bp)
        gla_p.append(sp)
        conv_s.append(bs)
        gla_s.append(ss)
    return (yp, ys, jnp.stack(conv_p), jnp.stack(gla_p), jnp.stack(conv_s), jnp.stack(gla_s))
```

```python
import functools
import math

import jax
import jax.numpy as jnp
import numpy as np
from jax import lax
from jax.experimental import pallas as pl
from jax.experimental.pallas import tpu as pltpu

F32 = jnp.float32
BF16 = jnp.bfloat16

D_MODEL = 1024
CONV_DIM = 512
CONV_W = 3
GLA_HEADS = 4
GLA_DK = 512
GLA_DV = 1024
GLA_HEAD_K = 128
GLA_HEAD_V = 256
GLA_GATE_RANK = 16
GLA_GATE_TAU = 16.0
PEER_HEADS = 8
PEER_NKEYS = 128
PEER_HALF = 128
PEER_TOPK = 16
N_MOD = 6
EPS = 1e-6
W_IN_MAIN = 3 * CONV_DIM + 2 * GLA_DK + 2 * GLA_DV + 2 * D_MODEL
OFF_BG, OFF_CG, OFF_HIN = 0, 512, 1024
OFF_Q, OFF_K, OFF_V, OFF_R, OFF_GA, OFF_GB = 1536, 2048, 2560, 3584, 4608, 5632

LANES = 128
SUBLANES = 8
VMEM_LIMIT_BYTES = 58 * 1024 * 1024

GLA_CHUNK = 128
MIX_TB_PROMPT = 256
MIX_SEQ_SAMPLE = 16
PEER_TB = 512
PEER_EC = 1024
PEER_I1 = PEER_EC // PEER_NKEYS
GELU_C0 = math.sqrt(2.0 / math.pi)
GELU_C1 = 0.044715


def _dot(a, b):
    return jnp.dot(a, b, preferred_element_type=F32)


def _dot_nt(a, b):
    return lax.dot_general(a, b, (((1,), (1,)), ((), ())), preferred_element_type=F32)


def _dot_tn(a, b):
    return lax.dot_general(a, b, (((0,), (0,)), ((), ())), preferred_element_type=F32)


def _split3(x):
    p0 = x.astype(BF16)
    r = x - p0.astype(F32)
    p1 = r.astype(BF16)
    r = r - p1.astype(F32)
    return p0, p1, r.astype(BF16)


def _rms(x, w):
    return x * lax.rsqrt(jnp.mean(x * x, axis=-1, keepdims=True) + EPS) * w


def _sigmoid(x):
    return 1.0 / (1.0 + jnp.exp(-x))


def _resident(shape, index_map):
    return pl.BlockSpec(shape, index_map, pipeline_mode=pl.Buffered(1))


def _rows_from_segments(seg_vals, seg_len):
    nseg, _, d = seg_vals.shape
    return jnp.broadcast_to(seg_vals, (nseg, seg_len, d)).reshape(nseg * seg_len, d)


def _ada_kernel(c_ref, w_ref, b_ref, o_ref):
    c = c_ref[...]
    a = (c * _sigmoid(c)).astype(BF16)
    o_ref[...] = _dot(a, w_ref[...].astype(BF16)) + b_ref[...]


def _ada_call(c_all, w_ada, b_ada):
    rows = c_all.shape[0]
    ncols = w_ada.shape[1]
    bn = 1536
    return pl.pallas_call(
        _ada_kernel,
        grid=(ncols // bn,),
        in_specs=[
            pl.BlockSpec((rows, D_MODEL), lambda j: (0, 0)),
            pl.BlockSpec((D_MODEL, bn), lambda j: (0, j)),
            pl.BlockSpec((1, bn), lambda j: (0, j)),
        ],
        out_specs=pl.BlockSpec((rows, bn), lambda j: (0, j)),
        out_shape=jax.ShapeDtypeStruct((rows, ncols), F32),
        compiler_params=pltpu.CompilerParams(
            dimension_semantics=("arbitrary",), vmem_limit_bytes=VMEM_LIMIT_BYTES),
        name="ada_mod",
    )(c_all, w_ada, b_ada.reshape(1, ncols))


def _gla_levels(seg_len):
    return [b for b in (1, 2, 4, 8, 16, 32, 64) if 2 * b <= seg_len]


def _gla_consts(chunk, seg_len):
    t = np.arange(chunk)
    seg = t // seg_len
    blocks = [(t[None, :] <= t[:, None]) & (seg[None, :] == seg[:, None])]
    masks = []
    for b in _gla_levels(seg_len):
        parent = t // (2 * b)
        refpos = parent * (2 * b) + b - 1
        upper = (t % (2 * b)) >= b
        a = np.zeros((chunk, chunk), bool)
        for i in range(chunk):
            if upper[i]:
                a[i, refpos[i] + 1:i + 1] = True
            else:
                a[i, i + 1:refpos[i] + 1] = True
        blocks.append(a)
        masks.append((parent[:, None] == parent[None, :]) & upper[:, None] & ~upper[None, :])
    segend = (seg + 1) * seg_len - 1
    a = np.zeros((chunk, chunk), bool)
    for i in range(chunk):
        a[i, i + 1:segend[i] + 1] = True
    blocks.append(a)
    masks.append(np.eye(chunk, dtype=bool))
    segones = np.zeros((chunk, LANES), np.float32)
    segones[t, seg] = 1.0
    return (jnp.asarray(np.concatenate(blocks, 0).astype(np.float32), BF16),
            jnp.asarray(np.stack(masks).astype(np.float32)),
            jnp.asarray(segones, BF16))


def _gla_chunk(q, k, v, la, a_all, masks, segones, states, seg_len):
    c = q.shape[0]
    nseg = c // seg_len
    nl = len(_gla_levels(seg_len))
    pieces = _split3(la)
    d_all = sum(_dot(a_all, p) for p in pieces)
    e_all = jnp.exp(d_all)
    cum_last_t = sum(_dot_tn(p, segones) for p in pieces)
    decay_cols = jnp.exp(cum_last_t)
    e_cum = e_all[0:c]
    e_last = e_all[(nl + 1) * c:(nl + 2) * c]
    outs = []
    new_states = [[None] * GLA_HEADS for _ in range(nseg)]
    for h in range(GLA_HEADS):
        ks = slice(h * GLA_HEAD_K, (h + 1) * GLA_HEAD_K)
        vs = slice(h * GLA_HEAD_V, (h + 1) * GLA_HEAD_V)
        qh, kh = q[:, ks], k[:, ks]
        vh = v[:, vs].astype(BF16)
        scores = _dot_nt(qh.astype(BF16), kh.astype(BF16)) * masks[nl]
        for li in range(nl):
            eb = e_all[(1 + li) * c:(2 + li) * c, ks]
            scores = scores + _dot_nt((qh * eb).astype(BF16), (kh * eb).astype(BF16)) * masks[li]
        o_h = _dot(scores.astype(BF16), vh)
        qt = qh * e_cum[:, ks]
        kt = kh * e_last[:, ks]
        vf = v[:, vs]
        inter = []
        for s in range(nseg):
            rows = slice(s * seg_len, (s + 1) * seg_len)
            st = states[s][h]
            inter.append(_dot(qt[rows].astype(BF16), st.astype(BF16)))
            new_states[s][h] = decay_cols[ks, s:s + 1] * st + _dot_tn(
                kt[rows].astype(BF16), vf[rows].astype(BF16))
        o_h = o_h + (inter[0] if nseg == 1 else jnp.concatenate(inter, axis=0))
        outs.append(o_h)
    return jnp.concatenate(outs, axis=1), new_states


def _mixer_kernel(cfg, x_ref, mod_ref, cst_ref, gst_ref, npre_ref, npost_ref, win_ref, wal_ref,
                  convw_ref, convb_ref, walpha_ref, balpha_ref, gnorm_ref, wpa_ref, wpb_ref,
                  wout_ref, aall_ref, masks_ref, segones_ref,
                  xo_ref, cnew_ref, gnew_ref, proj_ref, o_ref, carry_ref, state_ref):
    tb, nseq, carried = cfg["tb"], cfg["nseq"], cfg["carried"]
    lseq = tb // nseq
    chunk = min(GLA_CHUNK, tb)
    seg_len = min(lseq, chunk)
    step = pl.program_id(1)

    if carried:
        @pl.when(step == 0)
        def _():
            carry_ref[...] = jnp.zeros_like(carry_ref)
            state_ref[...] = jnp.zeros_like(state_ref)

    x = x_ref[...]
    mod = mod_ref[...]
    sh = _rows_from_segments(mod[:, :, 0:D_MODEL], lseq)
    sc = _rows_from_segments(mod[:, :, D_MODEL:2 * D_MODEL], lseq)
    gate = _rows_from_segments(mod[:, :, 2 * D_MODEL:3 * D_MODEL], lseq)
    h = (_rms(x, npre_ref[...]) * (1.0 + sc) + sh).astype(BF16)
    proj_ref[...] = _dot(h, win_ref[...])
    alow = _dot(h, wal_ref[...])

    u = proj_ref[:, OFF_CG:OFF_CG + CONV_DIM] * proj_ref[:, OFF_HIN:OFF_HIN + CONV_DIM]
    prev = carry_ref[...] if carried else cst_ref[...]
    prev0 = _rows_from_segments(prev[:, 0:1, :], lseq)
    prev1 = _rows_from_segments(prev[:, 1:2, :], lseq)
    pos = lax.broadcasted_iota(jnp.int32, (tb, 1), 0) % lseq
    u1 = jnp.where(pos == 0, prev1, pltpu.roll(u, 1, 0))
    u2 = jnp.where(pos == 0, prev0, jnp.where(pos == 1, prev1, pltpu.roll(u, 2, 0)))
    cw = convw_ref[...]
    y_conv = convb_ref[...] + cw[2:3, :] * u + cw[0:1, :] * u2 + cw[1:2, :] * u1
    new_buf = u.reshape(nseq, lseq, CONV_DIM)[:, lseq - 2:lseq, :]
    if carried:
        carry_ref[...] = new_buf
    cnew_ref[...] = new_buf
    branch_a = _dot((proj_ref[:, OFF_BG:OFF_BG + CONV_DIM] * y_conv).astype(BF16), wpa_ref[...])

    z = _dot(alow.astype(BF16), walpha_ref[...]) + balpha_ref[...]
    log_a = (jnp.minimum(z, 0.0) - jnp.log1p(jnp.exp(-jnp.abs(z)))) * (1.0 / GLA_GATE_TAU)
    a_all = aall_ref[...]
    masks = [masks_ref[i] for i in range(masks_ref.shape[0])]
    segones = segones_ref[...]
    nseg = chunk // seg_len
    for c in range(tb // chunk):
        rows = slice(c * chunk, (c + 1) * chunk)
        if carried:
            states = [[state_ref[hh] for hh in range(GLA_HEADS)]]
        else:
            states = [[gst_ref[c * nseg + s, hh] for hh in range(GLA_HEADS)] for s in range(nseg)]
        o_c, new_states = _gla_chunk(
            proj_ref[rows, OFF_Q:OFF_Q + GLA_DK] * (GLA_HEAD_K ** -0.5),
            proj_ref[rows, OFF_K:OFF_K + GLA_DK],
            proj_ref[rows, OFF_V:OFF_V + GLA_DV],
            log_a[rows], a_all, masks, segones, states, seg_len)
        o_ref[rows, :] = o_c
        for s in range(nseg):
            for hh in range(GLA_HEADS):
                if carried:
                    state_ref[hh] = new_states[s][hh]
                else:
                    gnew_ref[c * nseg + s, hh] = new_states[s][hh]
    if carried:
        gnew_ref[0] = state_ref[...]

    gn = gnorm_ref[...]
    normed = []
    for hh in range(GLA_HEADS):
        vs = slice(hh * GLA_HEAD_V, (hh + 1) * GLA_HEAD_V)
        normed.append(_rms(o_ref[:, vs], gn[:, vs]))
    r = proj_ref[:, OFF_R:OFF_R + GLA_DV]
    ob = jnp.concatenate(normed, axis=1) * (r * _sigmoid(r))
    branch_b = _dot(ob.astype(BF16), wpb_ref[...])

    merged = (_sigmoid(proj_ref[:, OFF_GA:OFF_GA + D_MODEL]) * branch_a
              + _sigmoid(proj_ref[:, OFF_GB:OFF_GB + D_MODEL]) * branch_b)
    m = _dot(merged.astype(BF16), wout_ref[...])
    xo_ref[...] = x + gate * _rms(m, npost_ref[...])


def _mixer_call(x2d, mod3, conv_state, gla_state, weights, *, nseq_total, seq_len, carried):
    if carried:
        tb, nseq = MIX_TB_PROMPT, 1
        grid = (nseq_total, seq_len // tb)
        x_map = lambda b, c: (b * (seq_len // tb) + c, 0)
    else:
        nseq = MIX_SEQ_SAMPLE
        tb = nseq * seq_len
        grid = (nseq_total // nseq, 1)
        x_map = lambda b, c: (b, 0)
    chunk = min(GLA_CHUNK, tb)
    seg_len = min(tb // nseq, chunk)
    a_all, masks, segones = _gla_consts(chunk, seg_len)
    cfg = dict(tb=tb, nseq=nseq, carried=carried)
    const2 = lambda b, c: (0, 0)
    const3 = lambda b, c: (0, 0, 0)
    (npre, npost, w_in_main, w_al, conv_w, conv_b, w_alpha, b_alpha, gnorm, wpa, wpb, wout) = weights
    n_tok = x2d.shape[0]
    state_spec = pl.BlockSpec(
        (nseq, GLA_HEADS, GLA_HEAD_K, GLA_HEAD_V), lambda b, c: (b, 0, 0, 0),
        **({} if carried else dict(pipeline_mode=pl.Buffered(1))))
    in_specs = [
        pl.BlockSpec((tb, D_MODEL), x_map),
        pl.BlockSpec((nseq, 1, 3 * D_MODEL), lambda b, c: (b, 0, 0)),
        pl.BlockSpec((nseq, CONV_W - 1, CONV_DIM), lambda b, c: (b, 0, 0)),
        state_spec,
        _resident((1, D_MODEL), const2),
        _resident((1, D_MODEL), const2),
        _resident((D_MODEL, W_IN_MAIN), const2),
        _resident((D_MODEL, LANES), const2),
        _resident((CONV_W, CONV_DIM), const2),
        _resident((1, CONV_DIM), const2),
        _resident((LANES, GLA_DK), const2),
        _resident((1, GLA_DK), const2),
        _resident((1, GLA_DV), const2),
        _resident((CONV_DIM, D_MODEL), const2),
        _resident((GLA_DV, D_MODEL), const2),
        _resident((D_MODEL, D_MODEL), const2),
        _resident(a_all.shape, const2),
        _resident(masks.shape, const3),
        _resident(segones.shape, const2),
    ]
    out_specs = [
        pl.BlockSpec((tb, D_MODEL), x_map),
        pl.BlockSpec((nseq, CONV_W - 1, CONV_DIM), lambda b, c: (b, 0, 0)),
        state_spec,
    ]
    out_shape = [
        jax.ShapeDtypeStruct((n_tok, D_MODEL), F32),
        jax.ShapeDtypeStruct((nseq_total, CONV_W - 1, CONV_DIM), F32),
        jax.ShapeDtypeStruct((nseq_total, GLA_HEADS, GLA_HEAD_K, GLA_HEAD_V), F32),
    ]
    scratch = [
        pltpu.VMEM((tb, W_IN_MAIN), F32),
        pltpu.VMEM((tb, GLA_DV), F32),
        pltpu.VMEM((1, CONV_W - 1, CONV_DIM), F32),
        pltpu.VMEM((GLA_HEADS, GLA_HEAD_K, GLA_HEAD_V), F32),
    ]
    return pl.pallas_call(
        functools.partial(_mixer_kernel, cfg),
        grid=grid,
        in_specs=in_specs,
        out_specs=out_specs,
        out_shape=out_shape,
        scratch_shapes=scratch,
        compiler_params=pltpu.CompilerParams(
            dimension_semantics=("arbitrary", "arbitrary"), vmem_limit_bytes=VMEM_LIMIT_BYTES),
        name="mixer_prompt" if carried else "mixer_sample",
    )(x2d, mod3, conv_state, gla_state, npre, npost, w_in_main, w_al, conv_w, conv_b, w_alpha,
      b_alpha, gnorm, wpa, wpb, wout, a_all, masks, segones)


def _wc_kernel(sk_ref, wq_ref, o_ref):
    o_ref[0] = lax.dot_general(sk_ref[0], wq_ref[...], (((1,), (1,)), ((), ())),
                               precision=lax.Precision.HIGHEST,
                               preferred_element_type=F32).astype(BF16)


def _wc_call(subkeys, wq):
    nhp = PEER_HEADS * 2
    out = pl.pallas_call(
        _wc_kernel,
        grid=(nhp,),
        in_specs=[
            pl.BlockSpec((1, PEER_NKEYS, PEER_HALF), lambda i: (i, 0, 0)),
            pl.BlockSpec((D_MODEL, PEER_HALF), lambda i: (0, i)),
        ],
        out_specs=pl.BlockSpec((1, PEER_NKEYS, D_MODEL), lambda i: (i, 0, 0)),
        out_shape=jax.ShapeDtypeStruct((nhp, PEER_NKEYS, D_MODEL), BF16),
        compiler_params=pltpu.CompilerParams(
            dimension_semantics=("arbitrary",), vmem_limit_bytes=VMEM_LIMIT_BYTES),
        name="peer_wc",
    )(subkeys.reshape(nhp, PEER_NKEYS, PEER_HALF), wq)
    return out.reshape(PEER_HEADS, 2, PEER_NKEYS, D_MODEL).transpose(1, 0, 2, 3).reshape(
        2 * PEER_HEADS * PEER_NKEYS, D_MODEL)


def _cand_pairs():
    n = PEER_TOPK + 1
    return [(a, b) for a in range(n) for b in range(n) if (a + 1) * (b + 1) <= n]


def _desc_distinct(vals, count):
    def tree_max(xs):
        xs = list(xs)
        while len(xs) > 1:
            xs = [jnp.maximum(xs[i], xs[i + 1]) if i + 1 < len(xs) else xs[i]
                  for i in range(0, len(xs), 2)]
        return xs[0]
    out = [tree_max(vals)]
    for _ in range(count - 1):
        m = out[-1]
        out.append(tree_max([jnp.where(v < m, v, -jnp.inf) for v in vals]))
    return out


def _peer_select(st_ref, c1_ref, e1_ref, e2_ref, lanes):
    nk = PEER_TOPK + 1
    tops = [[None] * PEER_HEADS for _ in range(2)]
    for p in range(2):
        for h in range(PEER_HEADS):
            r0 = (p * PEER_HEADS + h) * PEER_NKEYS
            s = st_ref[r0:r0 + PEER_NKEYS, lanes]
            out = [jnp.max(s, axis=0, keepdims=True)]
            for _ in range(nk - 1):
                out.append(jnp.max(jnp.where(s < out[-1], s, -jnp.inf), axis=0, keepdims=True))
            tops[p][h] = out
    packed = [[jnp.concatenate([tops[p][h][k] for h in range(PEER_HEADS)], axis=0)
               for k in range(nk)] for p in range(2)]
    cands = [packed[0][a] + packed[1][b] for (a, b) in _cand_pairs()]
    t = _desc_distinct(cands, nk)
    tau = 0.5 * (t[PEER_TOPK - 1] + t[PEER_TOPK])
    z = jnp.exp(t[0] - t[0])
    for k in range(1, PEER_TOPK):
        z = z + jnp.exp(t[k] - t[0])
    half_inv_z = 0.5 / z
    for h in range(PEER_HEADS):
        s1 = st_ref[h * PEER_NKEYS:(h + 1) * PEER_NKEYS, lanes]
        r2 = (PEER_HEADS + h) * PEER_NKEYS
        s2 = st_ref[r2:r2 + PEER_NKEYS, lanes]
        c1_ref[h, :, lanes] = tau[h:h + 1, :] - s1
        e1_ref[h, :, lanes] = jnp.exp(s1 - tops[0][h][0]) * half_inv_z[h:h + 1, :]
        e2_ref[h, :, lanes] = jnp.exp(s2 - tops[1][h][0])


def _peer_kernel(cfg, x_ref, mod_ref, npre_ref, npost_ref, wct_ref, u_ref, vt_ref, y_ref,
                 h2_ref, st_ref, c1_ref, e1_ref, e2_ref, a_ref, w_ref, acc_ref):
    tb, nseq = cfg["tb"], cfg["nseq"]
    lseq = tb // nseq
    e = pl.program_id(1)
    n_e = pl.num_programs(1)
    n_lg = tb // LANES

    @pl.when(e == 0)
    def _():
        mod = mod_ref[...]
        sh = _rows_from_segments(mod[:, :, 0:D_MODEL], lseq)
        sc = _rows_from_segments(mod[:, :, D_MODEL:2 * D_MODEL], lseq)
        h2 = (_rms(x_ref[...], npre_ref[...]) * (1.0 + sc) + sh).astype(BF16)
        h2_ref[...] = h2
        st_ref[...] = _dot_nt(wct_ref[...], h2)

        def body(lg, carry):
            lanes = pl.ds(pl.multiple_of(lg * LANES, LANES), LANES)
            _peer_select(st_ref, c1_ref, e1_ref, e2_ref, lanes)
            return carry
        lax.fori_loop(0, n_lg, body, 0)
        acc_ref[...] = jnp.zeros_like(acc_ref)

    a_ref[...] = _dot_nt(u_ref[...], h2_ref[...])
    i1_base = pl.multiple_of(e * PEER_I1, PEER_I1)

    def body(lg, carry):
        lanes = pl.ds(pl.multiple_of(lg * LANES, LANES), LANES)
        c1t = [c1_ref[h, pl.ds(i1_base, PEER_I1), lanes] for h in range(PEER_HEADS)]
        e1t = [e1_ref[h, pl.ds(i1_base, PEER_I1), lanes] for h in range(PEER_HEADS)]
        for i1 in range(PEER_I1):
            g = jnp.zeros((PEER_NKEYS, LANES), F32)
            for h in range(PEER_HEADS):
                r2 = (PEER_HEADS + h) * PEER_NKEYS
                s2 = st_ref[r2:r2 + PEER_NKEYS, lanes]
                p = e2_ref[h, :, lanes] * e1t[h][i1:i1 + 1, :]
                g = g + jnp.where(s2 >= c1t[h][i1:i1 + 1, :], p, 0.0)
            rows = slice(i1 * PEER_NKEYS, (i1 + 1) * PEER_NKEYS)
            a = a_ref[rows, lanes]
            act2 = a + a * jnp.tanh(GELU_C0 * (a + GELU_C1 * (a * a * a)))
            w_ref[rows, lanes] = (act2 * g).astype(BF16)
        return carry
    lax.fori_loop(0, n_lg, body, 0)
    acc_ref[...] += _dot(vt_ref[...], w_ref[...])

    @pl.when(e == n_e - 1)
    def _():
        mod = mod_ref[...]
        gate = _rows_from_segments(mod[:, :, 2 * D_MODEL:3 * D_MODEL], lseq)
        out = acc_ref[...].T
        y_ref[...] = x_ref[...] + gate * _rms(out, npost_ref[...])


def _peer_call(x2d, mod3, npre, npost, wct, u_bf, vt_bf, *, nseq_total, seq_len):
    n_tok = x2d.shape[0]
    tb = PEER_TB
    if seq_len >= tb:
        nseq = 1
        mod_map = lambda i, e: (i // (seq_len // tb), 0, 1)
    else:
        nseq = tb // seq_len
        mod_map = lambda i, e: (i, 0, 1)
    n_exp = u_bf.shape[0]
    cfg = dict(tb=tb, nseq=nseq)
    const2 = lambda i, e: (0, 0)
    return pl.pallas_call(
        functools.partial(_peer_kernel, cfg),
        grid=(n_tok // tb, n_exp // PEER_EC),
        in_specs=[
            pl.BlockSpec((tb, D_MODEL), lambda i, e: (i, 0)),
            pl.BlockSpec((nseq, 1, 3 * D_MODEL), mod_map),
            _resident((1, D_MODEL), const2),
            _resident((1, D_MODEL), const2),
            _resident(wct.shape, const2),
            pl.BlockSpec((PEER_EC, D_MODEL), lambda i, e: (e, 0)),
            pl.BlockSpec((D_MODEL, PEER_EC), lambda i, e: (0, e)),
        ],
        out_specs=pl.BlockSpec((tb, D_MODEL), lambda i, e: (i, 0)),
        out_shape=jax.ShapeDtypeStruct((n_tok, D_MODEL), F32),
        scratch_shapes=[
            pltpu.VMEM((tb, D_MODEL), BF16),
            pltpu.VMEM((2 * PEER_HEADS * PEER_NKEYS, tb), F32),
            pltpu.VMEM((PEER_HEADS, PEER_NKEYS, tb), F32),
            pltpu.VMEM((PEER_HEADS, PEER_NKEYS, tb), F32),
            pltpu.VMEM((PEER_HEADS, PEER_NKEYS, tb), F32),
            pltpu.VMEM((PEER_EC, tb), F32),
            pltpu.VMEM((PEER_EC, tb), BF16),
            pltpu.VMEM((D_MODEL, tb), F32),
        ],
        compiler_params=pltpu.CompilerParams(
            dimension_semantics=("arbitrary", "arbitrary"), vmem_limit_bytes=VMEM_LIMIT_BYTES),
        name="peer_prompt" if nseq == 1 else "peer_sample",
    )(x2d, mod3, npre, npost, wct, u_bf, vt_bf)


def kernel(x_prompt, x_sample, c_prompt, c_sample, state_conv, state_gla, norm_mix_pre, norm_mix_post, norm_ffn_pre, norm_ffn_post, w_ada, b_ada, w_in, conv_w, conv_b, w_alpha, b_alpha, gla_norm_w, w_proj_a, w_proj_b, w_out, peer_wq, peer_subkeys, peer_u, peer_v):
    depth = w_in.shape[0]
    assert depth == 1, "single-layer trunk"
    n_p, t_p, _ = x_prompt.shape
    n_s, t_s, _ = x_sample.shape
    l = 0

    c_all = jnp.concatenate([c_prompt, c_sample], axis=0)
    mod = _ada_call(c_all, w_ada[l], b_ada[l])
    mod_p = mod[:n_p].reshape(n_p, 1, N_MOD * D_MODEL)
    mod_s = mod[n_p:].reshape(n_s, 1, N_MOD * D_MODEL)

    w_in_main = w_in[l][:, :W_IN_MAIN].astype(BF16)
    w_al = jnp.pad(w_in[l][:, W_IN_MAIN:], ((0, 0), (0, LANES - GLA_GATE_RANK))).astype(BF16)
    w_alpha_p = jnp.pad(w_alpha[l], ((0, LANES - GLA_GATE_RANK), (0, 0))).astype(BF16)
    row = lambda a: a.reshape(1, -1)
    mixer_w = (row(norm_mix_pre[l]), row(norm_mix_post[l]), w_in_main, w_al, conv_w[l], row(conv_b[l]),
               w_alpha_p, row(b_alpha[l]), row(gla_norm_w[l]), w_proj_a[l].astype(BF16),
               w_proj_b[l].astype(BF16), w_out[l].astype(BF16))

    zeros_conv = jnp.zeros((n_p, CONV_W - 1, CONV_DIM), state_conv.dtype)
    zeros_gla = jnp.zeros((n_p, GLA_HEADS, GLA_HEAD_K, GLA_HEAD_V), state_gla.dtype)
    x1_p, conv_p, gla_p = _mixer_call(
        x_prompt.reshape(n_p * t_p, D_MODEL), mod_p, zeros_conv, zeros_gla, mixer_w,
        nseq_total=n_p, seq_len=t_p, carried=True)
    x1_s, conv_s, gla_s = _mixer_call(
        x_sample.reshape(n_s * t_s, D_MODEL), mod_s, state_conv[l], state_gla[l], mixer_w,
        nseq_total=n_s, seq_len=t_s, carried=False)

    wct = _wc_call(peer_subkeys[l], peer_wq[l])
    u_bf = peer_u[l].astype(BF16)
    vt_bf = peer_v[l].astype(BF16).T
    npre2, npost2 = row(norm_ffn_pre[l]), row(norm_ffn_post[l])
    y_p = _peer_call(x1_p, mod_p, npre2, npost2, wct, u_bf, vt_bf, nseq_total=n_p, seq_len=t_p)
    y_s = _peer_call(x1_s, mod_s, npre2, npost2, wct, u_bf, vt_bf, nseq_total=n_s, seq_len=t_s)

    return (y_p.reshape(n_p, t_p, D_MODEL), y_s.reshape(n_s, t_s, D_MODEL),
            conv_p[None], gla_p[None], conv_s[None], gla_s[None])
```

```python
import functools
import math

import jax
import jax.numpy as jnp
import numpy as np
from jax import lax
from jax.experimental import pallas as pl
from jax.experimental.pallas import tpu as pltpu

F32 = jnp.float32
BF16 = jnp.bfloat16

D_MODEL = 1024
CONV_DIM = 512
CONV_W = 3
GLA_HEADS = 4
GLA_DK = 512
GLA_DV = 1024
GLA_HEAD_K = 128
GLA_HEAD_V = 256
GLA_GATE_RANK = 16
GLA_GATE_TAU = 16.0
PEER_HEADS = 8
PEER_NKEYS = 128
PEER_HALF = 128
PEER_TOPK = 16
N_MOD = 6
EPS = 1e-6
W_IN_MAIN = 3 * CONV_DIM + 2 * GLA_DK + 2 * GLA_DV + 2 * D_MODEL
OFF_BG, OFF_CG, OFF_HIN = 0, 512, 1024
OFF_Q, OFF_K, OFF_V, OFF_R, OFF_GA, OFF_GB = 1536, 2048, 2560, 3584, 4608, 5632

LANES = 128
SUBLANES = 8
VMEM_LIMIT_BYTES = 58 * 1024 * 1024

GLA_CHUNK = 128
MIX_TB_PROMPT = 256
MIX_SEQ_SAMPLE = 16
PEER_TB = 512
PEER_EC = 1024
PEER_I1 = PEER_EC // PEER_NKEYS
GELU_C0 = math.sqrt(2.0 / math.pi)
GELU_C1 = 0.044715


def _dot(a, b):
    return jnp.dot(a, b, preferred_element_type=F32)


def _dot_nt(a, b):
    return lax.dot_general(a, b, (((1,), (1,)), ((), ())), preferred_element_type=F32)


def _dot_tn(a, b):
    return lax.dot_general(a, b, (((0,), (0,)), ((), ())), preferred_element_type=F32)


def _split3(x):
    p0 = x.astype(BF16)
    r = x - p0.astype(F32)
    p1 = r.astype(BF16)
    r = r - p1.astype(F32)
    return p0, p1, r.astype(BF16)


def _rms(x, w):
    return x * lax.rsqrt(jnp.mean(x * x, axis=-1, keepdims=True) + EPS) * w


def _sigmoid(x):
    return 1.0 / (1.0 + jnp.exp(-x))


def _resident(shape, index_map):
    return pl.BlockSpec(shape, index_map, pipeline_mode=pl.Buffered(1))


def _rows_from_segments(seg_vals, seg_len):
    nseg, _, d = seg_vals.shape
    return jnp.broadcast_to(seg_vals, (nseg, seg_len, d)).reshape(nseg * seg_len, d)


def _ada_kernel(c_ref, w_ref, b_ref, o_ref):
    c = c_ref[...]
    a = (c * _sigmoid(c)).astype(BF16)
    o_ref[...] = _dot(a, w_ref[...].astype(BF16)) + b_ref[...]


def _ada_call(c_all, w_ada, b_ada):
    rows = c_all.shape[0]
    ncols = w_ada.shape[1]
    bn = 1536
    return pl.pallas_call(
        _ada_kernel,
        grid=(ncols // bn,),
        in_specs=[
            pl.BlockSpec((rows, D_MODEL), lambda j: (0, 0)),
            pl.BlockSpec((D_MODEL, bn), lambda j: (0, j)),
            pl.BlockSpec((1, bn), lambda j: (0, j)),
        ],
        out_specs=pl.BlockSpec((rows, bn), lambda j: (0, j)),
        out_shape=jax.ShapeDtypeStruct((rows, ncols), F32),
        compiler_params=pltpu.CompilerParams(
            dimension_semantics=("arbitrary",), vmem_limit_bytes=VMEM_LIMIT_BYTES),
        name="ada_mod",
    )(c_all, w_ada, b_ada.reshape(1, ncols))


def _gla_levels(seg_len):
    return [b for b in (1, 2, 4, 8, 16, 32, 64) if 2 * b <= seg_len]


def _gla_consts(chunk, seg_len):
    t = np.arange(chunk)
    seg = t // seg_len
    blocks = [(t[None, :] <= t[:, None]) & (seg[None, :] == seg[:, None])]
    masks = []
    for b in _gla_levels(seg_len):
        parent = t // (2 * b)
        refpos = parent * (2 * b) + b - 1
        upper = (t % (2 * b)) >= b
        a = np.zeros((chunk, chunk), bool)
        for i in range(chunk):
            if upper[i]:
                a[i, refpos[i] + 1:i + 1] = True
            else:
                a[i, i + 1:refpos[i] + 1] = True
        blocks.append(a)
        masks.append((parent[:, None] == parent[None, :]) & upper[:, None] & ~upper[None, :])
    segend = (seg + 1) * seg_len - 1
    a = np.zeros((chunk, chunk), bool)
    for i in range(chunk):
        a[i, i + 1:segend[i] + 1] = True
    blocks.append(a)
    masks.append(np.eye(chunk, dtype=bool))
    segones = np.zeros((chunk, LANES), np.float32)
    segones[t, seg] = 1.0
    return (jnp.asarray(np.concatenate(blocks, 0).astype(np.float32), BF16),
            jnp.asarray(np.stack(masks).astype(np.float32)),
            jnp.asarray(segones, BF16))


def _gla_chunk(q, k, v, la, a_all, masks, segones, states, seg_len):
    c = q.shape[0]
    nseg = c // seg_len
    nl = len(_gla_levels(seg_len))
    pieces = _split3(la)
    d_all = sum(_dot(a_all, p) for p in pieces)
    e_all = jnp.exp(d_all)
    cum_last_t = sum(_dot_tn(p, segones) for p in pieces)
    decay_cols = jnp.exp(cum_last_t)
    e_cum = e_all[0:c]
    e_last = e_all[(nl + 1) * c:(nl + 2) * c]
    outs = []
    new_states = [[None] * GLA_HEADS for _ in range(nseg)]
    for h in range(GLA_HEADS):
        ks = slice(h * GLA_HEAD_K, (h + 1) * GLA_HEAD_K)
        vs = slice(h * GLA_HEAD_V, (h + 1) * GLA_HEAD_V)
        qh, kh = q[:, ks], k[:, ks]
        vh = v[:, vs].astype(BF16)
        scores = _dot_nt(qh.astype(BF16), kh.astype(BF16)) * masks[nl]
        for li in range(nl):
            eb = e_all[(1 + li) * c:(2 + li) * c, ks]
            scores = scores + _dot_nt((qh * eb).astype(BF16), (kh * eb).astype(BF16)) * masks[li]
        o_h = _dot(scores.astype(BF16), vh)
        qt = qh * e_cum[:, ks]
        kt = kh * e_last[:, ks]
        vf = v[:, vs]
        inter = []
        for s in range(nseg):
            rows = slice(s * seg_len, (s + 1) * seg_len)
            st = states[s][h]
            inter.append(_dot(qt[rows].astype(BF16), st.astype(BF16)))
            new_states[s][h] = decay_cols[ks, s:s + 1] * st + _dot_tn(
                kt[rows].astype(BF16), vf[rows].astype(BF16))
        o_h = o_h + (inter[0] if nseg == 1 else jnp.concatenate(inter, axis=0))
        outs.append(o_h)
    return jnp.concatenate(outs, axis=1), new_states


def _mixer_kernel(cfg, x_ref, mod_ref, cst_ref, gst_ref, npre_ref, npost_ref, win_ref, wal_ref,
                  convw_ref, convb_ref, walpha_ref, balpha_ref, gnorm_ref, wpa_ref, wpb_ref,
                  wout_ref, aall_ref, masks_ref, segones_ref,
                  xo_ref, cnew_ref, gnew_ref, proj_ref, o_ref, carry_ref, state_ref):
    tb, nseq, carried = cfg["tb"], cfg["nseq"], cfg["carried"]
    lseq = tb // nseq
    chunk = min(GLA_CHUNK, tb)
    seg_len = min(lseq, chunk)
    step = pl.program_id(1)

    if carried:
        @pl.when(step == 0)
        def _():
            carry_ref[...] = jnp.zeros_like(carry_ref)
            state_ref[...] = jnp.zeros_like(state_ref)

    x = x_ref[...]
    mod = mod_ref[...]
    sh = _rows_from_segments(mod[:, :, 0:D_MODEL], lseq)
    sc = _rows_from_segments(mod[:, :, D_MODEL:2 * D_MODEL], lseq)
    gate = _rows_from_segments(mod[:, :, 2 * D_MODEL:3 * D_MODEL], lseq)
    h = (_rms(x, npre_ref[...]) * (1.0 + sc) + sh).astype(BF16)
    proj_ref[...] = _dot(h, win_ref[...])
    alow = _dot(h, wal_ref[...])

    u = proj_ref[:, OFF_CG:OFF_CG + CONV_DIM] * proj_ref[:, OFF_HIN:OFF_HIN + CONV_DIM]
    prev = carry_ref[...] if carried else cst_ref[...]
    prev0 = _rows_from_segments(prev[:, 0:1, :], lseq)
    prev1 = _rows_from_segments(prev[:, 1:2, :], lseq)
    pos = lax.broadcasted_iota(jnp.int32, (tb, 1), 0) % lseq
    u1 = jnp.where(pos == 0, prev1, pltpu.roll(u, 1, 0))
    u2 = jnp.where(pos == 0, prev0, jnp.where(pos == 1, prev1, pltpu.roll(u, 2, 0)))
    cw = convw_ref[...]
    y_conv = convb_ref[...] + cw[2:3, :] * u + cw[0:1, :] * u2 + cw[1:2, :] * u1
    new_buf = u.reshape(nseq, lseq, CONV_DIM)[:, lseq - 2:lseq, :]
    if carried:
        carry_ref[...] = new_buf
    cnew_ref[...] = new_buf
    branch_a = _dot((proj_ref[:, OFF_BG:OFF_BG + CONV_DIM] * y_conv).astype(BF16), wpa_ref[...])

    z = _dot(alow.astype(BF16), walpha_ref[...]) + balpha_ref[...]
    log_a = (jnp.minimum(z, 0.0) - jnp.log1p(jnp.exp(-jnp.abs(z)))) * (1.0 / GLA_GATE_TAU)
    a_all = aall_ref[...]
    masks = [masks_ref[i] for i in range(masks_ref.shape[0])]
    segones = segones_ref[...]
    nseg = chunk // seg_len
    for c in range(tb // chunk):
        rows = slice(c * chunk, (c + 1) * chunk)
        if carried:
            states = [[state_ref[hh] for hh in range(GLA_HEADS)]]
        else:
            states = [[gst_ref[c * nseg + s, hh] for hh in range(GLA_HEADS)] for s in range(nseg)]
        o_c, new_states = _gla_chunk(
            proj_ref[rows, OFF_Q:OFF_Q + GLA_DK] * (GLA_HEAD_K ** -0.5),
            proj_ref[rows, OFF_K:OFF_K + GLA_DK],
            proj_ref[rows, OFF_V:OFF_V + GLA_DV],
            log_a[rows], a_all, masks, segones, states, seg_len)
        o_ref[rows, :] = o_c
        for s in range(nseg):
            for hh in range(GLA_HEADS):
                if carried:
                    state_ref[hh] = new_states[s][hh]
                else:
                    gnew_ref[c * nseg + s, hh] = new_states[s][hh]
    if carried:
        gnew_ref[0] = state_ref[...]

    gn = gnorm_ref[...]
    normed = []
    for hh in range(GLA_HEADS):
        vs = slice(hh * GLA_HEAD_V, (hh + 1) * GLA_HEAD_V)
        normed.append(_rms(o_ref[:, vs], gn[:, vs]))
    r = proj_ref[:, OFF_R:OFF_R + GLA_DV]
    ob = jnp.concatenate(normed, axis=1) * (r * _sigmoid(r))
    branch_b = _dot(ob.astype(BF16), wpb_ref[...])

    merged = (_sigmoid(proj_ref[:, OFF_GA:OFF_GA + D_MODEL]) * branch_a
              + _sigmoid(proj_ref[:, OFF_GB:OFF_GB + D_MODEL]) * branch_b)
    m = _dot(merged.astype(BF16), wout_ref[...])
    xo_ref[...] = x + gate * _rms(m, npost_ref[...])


def _mixer_call(x2d, mod3, conv_state, gla_state, weights, *, nseq_total, seq_len, carried):
    if carried:
        tb, nseq = MIX_TB_PROMPT, 1
        grid = (nseq_total, seq_len // tb)
        x_map = lambda b, c: (b * (seq_len // tb) + c, 0)
    else:
        nseq = MIX_SEQ_SAMPLE
        tb = nseq * seq_len
        grid = (nseq_total // nseq, 1)
        x_map = lambda b, c: (b, 0)
    chunk = min(GLA_CHUNK, tb)
    seg_len = min(tb // nseq, chunk)
    a_all, masks, segones = _gla_consts(chunk, seg_len)
    cfg = dict(tb=tb, nseq=nseq, carried=carried)
    const2 = lambda b, c: (0, 0)
    const3 = lambda b, c: (0, 0, 0)
    (npre, npost, w_in_main, w_al, conv_w, conv_b, w_alpha, b_alpha, gnorm, wpa, wpb, wout) = weights
    n_tok = x2d.shape[0]
    state_spec = pl.BlockSpec(
        (nseq, GLA_HEADS, GLA_HEAD_K, GLA_HEAD_V), lambda b, c: (b, 0, 0, 0),
        **({} if carried else dict(pipeline_mode=pl.Buffered(1))))
    in_specs = [
        pl.BlockSpec((tb, D_MODEL), x_map),
        pl.BlockSpec((nseq, 1, 3 * D_MODEL), lambda b, c: (b, 0, 0)),
        pl.BlockSpec((nseq, CONV_W - 1, CONV_DIM), lambda b, c: (b, 0, 0)),
        state_spec,
        _resident((1, D_MODEL), const2),
        _resident((1, D_MODEL), const2),
        _resident((D_MODEL, W_IN_MAIN), const2),
        _resident((D_MODEL, LANES), const2),
        _resident((CONV_W, CONV_DIM), const2),
        _resident((1, CONV_DIM), const2),
        _resident((LANES, GLA_DK), const2),
        _resident((1, GLA_DK), const2),
        _resident((1, GLA_DV), const2),
        _resident((CONV_DIM, D_MODEL), const2),
        _resident((GLA_DV, D_MODEL), const2),
        _resident((D_MODEL, D_MODEL), const2),
        _resident(a_all.shape, const2),
        _resident(masks.shape, const3),
        _resident(segones.shape, const2),
    ]
    out_specs = [
        pl.BlockSpec((tb, D_MODEL), x_map),
        pl.BlockSpec((nseq, CONV_W - 1, CONV_DIM), lambda b, c: (b, 0, 0)),
        state_spec,
    ]
    out_shape = [
        jax.ShapeDtypeStruct((n_tok, D_MODEL), F32),
        jax.ShapeDtypeStruct((nseq_total, CONV_W - 1, CONV_DIM), F32),
        jax.ShapeDtypeStruct((nseq_total, GLA_HEADS, GLA_HEAD_K, GLA_HEAD_V), F32),
    ]
    scratch = [
        pltpu.VMEM((tb, W_IN_MAIN), F32),
        pltpu.VMEM((tb, GLA_DV), F32),
        pltpu.VMEM((1, CONV_W - 1, CONV_DIM), F32),
        pltpu.VMEM((GLA_HEADS, GLA_HEAD_K, GLA_HEAD_V), F32),
    ]
    return pl.pallas_call(
        functools.partial(_mixer_kernel, cfg),
        grid=grid,
        in_specs=in_specs,
        out_specs=out_specs,
        out_shape=out_shape,
        scratch_shapes=scratch,
        compiler_params=pltpu.CompilerParams(
            dimension_semantics=("arbitrary", "arbitrary"), vmem_limit_bytes=VMEM_LIMIT_BYTES),
        name="mixer_prompt" if carried else "mixer_sample",
    )(x2d, mod3, conv_state, gla_state, npre, npost, w_in_main, w_al, conv_w, conv_b, w_alpha,
      b_alpha, gnorm, wpa, wpb, wout, a_all, masks, segones)


def _wc_kernel(sk_ref, wq_ref, o_ref):
    o_ref[0] = lax.dot_general(sk_ref[0], wq_ref[...], (((1,), (1,)), ((), ())),
                               precision=lax.Precision.HIGHEST,
                               preferred_element_type=F32).astype(BF16)


def _wc_call(subkeys, wq):
    nhp = PEER_HEADS * 2
    out = pl.pallas_call(
        _wc_kernel,
        grid=(nhp,),
        in_specs=[
            pl.BlockSpec((1, PEER_NKEYS, PEER_HALF), lambda i: (i, 0, 0)),
            pl.BlockSpec((D_MODEL, PEER_HALF), lambda i: (0, i)),
        ],
        out_specs=pl.BlockSpec((1, PEER_NKEYS, D_MODEL), lambda i: (i, 0, 0)),
        out_shape=jax.ShapeDtypeStruct((nhp, PEER_NKEYS, D_MODEL), BF16),
        compiler_params=pltpu.CompilerParams(
            dimension_semantics=("arbitrary",), vmem_limit_bytes=VMEM_LIMIT_BYTES),
        name="peer_wc",
    )(subkeys.reshape(nhp, PEER_NKEYS, PEER_HALF), wq)
    return out.reshape(PEER_HEADS, 2, PEER_NKEYS, D_MODEL).transpose(1, 0, 2, 3).reshape(
        2 * PEER_HEADS * PEER_NKEYS, D_MODEL)


def _cand_pairs():
    n = PEER_TOPK + 1
    return [(a, b) for a in range(n) for b in range(n) if (a + 1) * (b + 1) <= n]


def _desc_distinct(vals, count):
    def tree_max(xs):
        xs = list(xs)
        while len(xs) > 1:
            xs = [jnp.maximum(xs[i], xs[i + 1]) if i + 1 < len(xs) else xs[i]
                  for i in range(0, len(xs), 2)]
        return xs[0]
    out = [tree_max(vals)]
    for _ in range(count - 1):
        m = out[-1]
        out.append(tree_max([jnp.where(v < m, v, -jnp.inf) for v in vals]))
    return out


def _peer_select(st_ref, c1_ref, e1_ref, e2_ref, lg):
    nk = PEER_TOPK + 1
    tops = [[None] * PEER_HEADS for _ in range(2)]
    for p in range(2):
        for h in range(PEER_HEADS):
            r0 = (p * PEER_HEADS + h) * PEER_NKEYS
            s = st_ref[lg, r0:r0 + PEER_NKEYS, :]
            out = [jnp.max(s, axis=0, keepdims=True)]
            for _ in range(nk - 1):
                out.append(jnp.max(jnp.where(s < out[-1], s, -jnp.inf), axis=0, keepdims=True))
            tops[p][h] = out
    packed = [[jnp.concatenate([tops[p][h][k] for h in range(PEER_HEADS)], axis=0)
               for k in range(nk)] for p in range(2)]
    cands = [packed[0][a] + packed[1][b] for (a, b) in _cand_pairs()]
    t = _desc_distinct(cands, nk)
    tau = 0.5 * (t[PEER_TOPK - 1] + t[PEER_TOPK])
    z = jnp.exp(t[0] - t[0])
    for k in range(1, PEER_TOPK):
        z = z + jnp.exp(t[k] - t[0])
    half_inv_z = 0.5 / z
    for h in range(PEER_HEADS):
        s1 = st_ref[lg, h * PEER_NKEYS:(h + 1) * PEER_NKEYS, :]
        r2 = (PEER_HEADS + h) * PEER_NKEYS
        s2 = st_ref[lg, r2:r2 + PEER_NKEYS, :]
        c1_ref[lg, h] = tau[h:h + 1, :] - s1
        e1_ref[lg, h] = jnp.exp(s1 - tops[0][h][0]) * half_inv_z[h:h + 1, :]
        e2_ref[lg, h] = jnp.exp(s2 - tops[1][h][0])


def _peer_kernel(cfg, x_ref, mod_ref, npre_ref, npost_ref, wct_ref, u_ref, vt_ref, y_ref,
                 h2_ref, st_ref, c1_ref, e1_ref, e2_ref, a_ref, w_ref, acc_ref):
    tb, nseq = cfg["tb"], cfg["nseq"]
    lseq = tb // nseq
    e = pl.program_id(1)
    n_e = pl.num_programs(1)
    n_lg = tb // LANES

    @pl.when(e == 0)
    def _():
        mod = mod_ref[...]
        sh = _rows_from_segments(mod[:, :, 0:D_MODEL], lseq)
        sc = _rows_from_segments(mod[:, :, D_MODEL:2 * D_MODEL], lseq)
        h2 = (_rms(x_ref[...], npre_ref[...]) * (1.0 + sc) + sh).astype(BF16)
        h2_ref[...] = h2
        st = _dot_nt(wct_ref[...], h2)
        for lg in range(n_lg):
            st_ref[lg] = st[:, lg * LANES:(lg + 1) * LANES]

        def body(lg, carry):
            _peer_select(st_ref, c1_ref, e1_ref, e2_ref, lg)
            return carry
        lax.fori_loop(0, n_lg, body, 0)
        acc_ref[...] = jnp.zeros_like(acc_ref)

    a_all = _dot_nt(u_ref[...], h2_ref[...])
    for lg in range(n_lg):
        a_ref[lg] = a_all[:, lg * LANES:(lg + 1) * LANES]
    i1_base = pl.multiple_of(e * PEER_I1, PEER_I1)

    def body(lg, carry):
        lanes = pl.ds(pl.multiple_of(lg * LANES, LANES), LANES)
        c1t = [c1_ref[lg, h, pl.ds(i1_base, PEER_I1), :] for h in range(PEER_HEADS)]
        e1t = [e1_ref[lg, h, pl.ds(i1_base, PEER_I1), :] for h in range(PEER_HEADS)]
        for i1 in range(PEER_I1):
            g = jnp.zeros((PEER_NKEYS, LANES), F32)
            for h in range(PEER_HEADS):
                r2 = (PEER_HEADS + h) * PEER_NKEYS
                s2 = st_ref[lg, r2:r2 + PEER_NKEYS, :]
                p = e2_ref[lg, h] * e1t[h][i1:i1 + 1, :]
                g = g + jnp.where(s2 >= c1t[h][i1:i1 + 1, :], p, 0.0)
            rows = slice(i1 * PEER_NKEYS, (i1 + 1) * PEER_NKEYS)
            a = a_ref[lg, rows, :]
            act2 = a + a * jnp.tanh(GELU_C0 * (a + GELU_C1 * (a * a * a)))
            w_ref[rows, lanes] = (act2 * g).astype(BF16)
        return carry
    lax.fori_loop(0, n_lg, body, 0)
    acc_ref[...] += _dot(vt_ref[...], w_ref[...])

    @pl.when(e == n_e - 1)
    def _():
        mod = mod_ref[...]
        gate = _rows_from_segments(mod[:, :, 2 * D_MODEL:3 * D_MODEL], lseq)
        out = acc_ref[...].T
        y_ref[...] = x_ref[...] + gate * _rms(out, npost_ref[...])


def _peer_call(x2d, mod3, npre, npost, wct, u_bf, vt_bf, *, nseq_total, seq_len):
    n_tok = x2d.shape[0]
    tb = PEER_TB
    if seq_len >= tb:
        nseq = 1
        mod_map = lambda i, e: (i // (seq_len // tb), 0, 1)
    else:
        nseq = tb // seq_len
        mod_map = lambda i, e: (i, 0, 1)
    n_exp = u_bf.shape[0]
    cfg = dict(tb=tb, nseq=nseq)
    const2 = lambda i, e: (0, 0)
    return pl.pallas_call(
        functools.partial(_peer_kernel, cfg),
        grid=(n_tok // tb, n_exp // PEER_EC),
        in_specs=[
            pl.BlockSpec((tb, D_MODEL), lambda i, e: (i, 0)),
            pl.BlockSpec((nseq, 1, 3 * D_MODEL), mod_map),
            _resident((1, D_MODEL), const2),
            _resident((1, D_MODEL), const2),
            _resident(wct.shape, const2),
            pl.BlockSpec((PEER_EC, D_MODEL), lambda i, e: (e, 0)),
            pl.BlockSpec((D_MODEL, PEER_EC), lambda i, e: (0, e)),
        ],
        out_specs=pl.BlockSpec((tb, D_MODEL), lambda i, e: (i, 0)),
        out_shape=jax.ShapeDtypeStruct((n_tok, D_MODEL), F32),
        scratch_shapes=[
            pltpu.VMEM((tb, D_MODEL), BF16),
            pltpu.VMEM((tb // LANES, 2 * PEER_HEADS * PEER_NKEYS, LANES), F32),
            pltpu.VMEM((tb // LANES, PEER_HEADS, PEER_NKEYS, LANES), F32),
            pltpu.VMEM((tb // LANES, PEER_HEADS, PEER_NKEYS, LANES), F32),
            pltpu.VMEM((tb // LANES, PEER_HEADS, PEER_NKEYS, LANES), F32),
            pltpu.VMEM((tb // LANES, PEER_EC, LANES), F32),
            pltpu.VMEM((PEER_EC, tb), BF16),
            pltpu.VMEM((D_MODEL, tb), F32),
        ],
        compiler_params=pltpu.CompilerParams(
            dimension_semantics=("arbitrary", "arbitrary"), vmem_limit_bytes=VMEM_LIMIT_BYTES),
        name="peer_prompt" if nseq == 1 else "peer_sample",
    )(x2d, mod3, npre, npost, wct, u_bf, vt_bf)


def kernel(x_prompt, x_sample, c_prompt, c_sample, state_conv, state_gla, norm_mix_pre, norm_mix_post, norm_ffn_pre, norm_ffn_post, w_ada, b_ada, w_in, conv_w, conv_b, w_alpha, b_alpha, gla_norm_w, w_proj_a, w_proj_b, w_out, peer_wq, peer_subkeys, peer_u, peer_v):
    depth = w_in.shape[0]
    assert depth == 1, "single-layer trunk"
    n_p, t_p, _ = x_prompt.shape
    n_s, t_s, _ = x_sample.shape
    l = 0

    c_all = jnp.concatenate([c_prompt, c_sample], axis=0)
    mod = _ada_call(c_all, w_ada[l], b_ada[l])
    mod_p = mod[:n_p].reshape(n_p, 1, N_MOD * D_MODEL)
    mod_s = mod[n_p:].reshape(n_s, 1, N_MOD * D_MODEL)

    w_in_main = w_in[l][:, :W_IN_MAIN].astype(BF16)
    w_al = jnp.pad(w_in[l][:, W_IN_MAIN:], ((0, 0), (0, LANES - GLA_GATE_RANK))).astype(BF16)
    w_alpha_p = jnp.pad(w_alpha[l], ((0, LANES - GLA_GATE_RANK), (0, 0))).astype(BF16)
    row = lambda a: a.reshape(1, -1)
    mixer_w = (row(norm_mix_pre[l]), row(norm_mix_post[l]), w_in_main, w_al, conv_w[l], row(conv_b[l]),
               w_alpha_p, row(b_alpha[l]), row(gla_norm_w[l]), w_proj_a[l].astype(BF16),
               w_proj_b[l].astype(BF16), w_out[l].astype(BF16))

    zeros_conv = jnp.zeros((n_p, CONV_W - 1, CONV_DIM), state_conv.dtype)
    zeros_gla = jnp.zeros((n_p, GLA_HEADS, GLA_HEAD_K, GLA_HEAD_V), state_gla.dtype)
    x1_p, conv_p, gla_p = _mixer_call(
        x_prompt.reshape(n_p * t_p, D_MODEL), mod_p, zeros_conv, zeros_gla, mixer_w,
        nseq_total=n_p, seq_len=t_p, carried=True)
    x1_s, conv_s, gla_s = _mixer_call(
        x_sample.reshape(n_s * t_s, D_MODEL), mod_s, state_conv[l], state_gla[l], mixer_w,
        nseq_total=n_s, seq_len=t_s, carried=False)

    wct = _wc_call(peer_subkeys[l], peer_wq[l])
    u_bf = peer_u[l].astype(BF16)
    vt_bf = peer_v[l].astype(BF16).T
    npre2, npost2 = row(norm_ffn_pre[l]), row(norm_ffn_post[l])
    y_p = _peer_call(x1_p, mod_p, npre2, npost2, wct, u_bf, vt_bf, nseq_total=n_p, seq_len=t_p)
    y_s = _peer_call(x1_s, mod_s, npre2, npost2, wct, u_bf, vt_bf, nseq_total=n_s, seq_len=t_s)

    return (y_p.reshape(n_p, t_p, D_MODEL), y_s.reshape(n_s, t_s, D_MODEL),
            conv_p[None], gla_p[None], conv_s[None], gla_s[None])
```

```python
import functools
import math

import jax
import jax.numpy as jnp
import numpy as np
from jax import lax
from jax.experimental import pallas as pl
from jax.experimental.pallas import tpu as pltpu

F32 = jnp.float32
BF16 = jnp.bfloat16

D_MODEL = 1024
CONV_DIM = 512
CONV_W = 3
GLA_HEADS = 4
GLA_DK = 512
GLA_DV = 1024
GLA_HEAD_K = 128
GLA_HEAD_V = 256
GLA_GATE_RANK = 16
GLA_GATE_TAU = 16.0
PEER_HEADS = 8
PEER_NKEYS = 128
PEER_HALF = 128
PEER_TOPK = 16
N_MOD = 6
EPS = 1e-6
W_IN_MAIN = 3 * CONV_DIM + 2 * GLA_DK + 2 * GLA_DV + 2 * D_MODEL
OFF_BG, OFF_CG, OFF_HIN = 0, 512, 1024
OFF_Q, OFF_K, OFF_V, OFF_R, OFF_GA, OFF_GB = 1536, 2048, 2560, 3584, 4608, 5632

LANES = 128
SUBLANES = 8
VMEM_LIMIT_BYTES = 58 * 1024 * 1024

GLA_CHUNK = 128
MIX_TB_PROMPT = 256
MIX_SEQ_SAMPLE = 16
PEER_TB = 512
PEER_EC = 2048
PEER_SLICE = 512
PEER_I1 = PEER_EC // PEER_NKEYS
GELU_C0 = math.sqrt(2.0 / math.pi)
GELU_C1 = 0.044715


def _dot(a, b):
    return jnp.dot(a, b, preferred_element_type=F32)


def _dot_nt(a, b):
    return lax.dot_general(a, b, (((1,), (1,)), ((), ())), preferred_element_type=F32)


def _dot_tn(a, b):
    return lax.dot_general(a, b, (((0,), (0,)), ((), ())), preferred_element_type=F32)


def _split3(x):
    p0 = x.astype(BF16)
    r = x - p0.astype(F32)
    p1 = r.astype(BF16)
    r = r - p1.astype(F32)
    return p0, p1, r.astype(BF16)


def _rms(x, w):
    return x * lax.rsqrt(jnp.mean(x * x, axis=-1, keepdims=True) + EPS) * w


def _sigmoid(x):
    return 1.0 / (1.0 + jnp.exp(-x))


def _resident(shape, index_map):
    return pl.BlockSpec(shape, index_map, pipeline_mode=pl.Buffered(1))


def _rows_from_segments(seg_vals, seg_len):
    nseg, _, d = seg_vals.shape
    return jnp.broadcast_to(seg_vals, (nseg, seg_len, d)).reshape(nseg * seg_len, d)


def _ada_kernel(c_ref, w_ref, b_ref, o_ref):
    c = c_ref[...]
    a = (c * _sigmoid(c)).astype(BF16)
    o_ref[...] = _dot(a, w_ref[...].astype(BF16)) + b_ref[...]


def _ada_call(c_all, w_ada, b_ada):
    rows = c_all.shape[0]
    ncols = w_ada.shape[1]
    bn = 1536
    return pl.pallas_call(
        _ada_kernel,
        grid=(ncols // bn,),
        in_specs=[
            pl.BlockSpec((rows, D_MODEL), lambda j: (0, 0)),
            pl.BlockSpec((D_MODEL, bn), lambda j: (0, j)),
            pl.BlockSpec((1, bn), lambda j: (0, j)),
        ],
        out_specs=pl.BlockSpec((rows, bn), lambda j: (0, j)),
        out_shape=jax.ShapeDtypeStruct((rows, ncols), F32),
        compiler_params=pltpu.CompilerParams(
            dimension_semantics=("arbitrary",), vmem_limit_bytes=VMEM_LIMIT_BYTES),
        name="ada_mod",
    )(c_all, w_ada, b_ada.reshape(1, ncols))


def _gla_levels(seg_len):
    return [b for b in (1, 2, 4, 8, 16, 32, 64) if 2 * b <= seg_len]


def _gla_consts(chunk, seg_len):
    t = np.arange(chunk)
    seg = t // seg_len
    blocks = [(t[None, :] <= t[:, None]) & (seg[None, :] == seg[:, None])]
    masks = []
    for b in _gla_levels(seg_len):
        parent = t // (2 * b)
        refpos = parent * (2 * b) + b - 1
        upper = (t % (2 * b)) >= b
        a = np.zeros((chunk, chunk), bool)
        for i in range(chunk):
            if upper[i]:
                a[i, refpos[i] + 1:i + 1] = True
            else:
                a[i, i + 1:refpos[i] + 1] = True
        blocks.append(a)
        masks.append((parent[:, None] == parent[None, :]) & upper[:, None] & ~upper[None, :])
    segend = (seg + 1) * seg_len - 1
    a = np.zeros((chunk, chunk), bool)
    for i in range(chunk):
        a[i, i + 1:segend[i] + 1] = True
    blocks.append(a)
    masks.append(np.eye(chunk, dtype=bool))
    segones = np.zeros((chunk, LANES), np.float32)
    segones[t, seg] = 1.0
    return (jnp.asarray(np.concatenate(blocks, 0).astype(np.float32), BF16),
            jnp.asarray(np.stack(masks).astype(np.float32)),
            jnp.asarray(segones, BF16))


def _gla_chunk(q, k, v, la, a_all, masks, segones, states, seg_len):
    c = q.shape[0]
    nseg = c // seg_len
    nl = len(_gla_levels(seg_len))
    pieces = _split3(la)
    d_all = sum(_dot(a_all, p) for p in pieces)
    e_all = jnp.exp(d_all)
    cum_last_t = sum(_dot_tn(p, segones) for p in pieces)
    decay_cols = jnp.exp(cum_last_t)
    e_cum = e_all[0:c]
    e_last = e_all[(nl + 1) * c:(nl + 2) * c]
    outs = []
    new_states = [[None] * GLA_HEADS for _ in range(nseg)]
    for h in range(GLA_HEADS):
        ks = slice(h * GLA_HEAD_K, (h + 1) * GLA_HEAD_K)
        vs = slice(h * GLA_HEAD_V, (h + 1) * GLA_HEAD_V)
        qh, kh = q[:, ks], k[:, ks]
        vh = v[:, vs].astype(BF16)
        scores = _dot_nt(qh.astype(BF16), kh.astype(BF16)) * masks[nl]
        for li in range(nl):
            eb = e_all[(1 + li) * c:(2 + li) * c, ks]
            scores = scores + _dot_nt((qh * eb).astype(BF16), (kh * eb).astype(BF16)) * masks[li]
        o_h = _dot(scores.astype(BF16), vh)
        qt = qh * e_cum[:, ks]
        kt = kh * e_last[:, ks]
        vf = v[:, vs]
        inter = []
        for s in range(nseg):
            rows = slice(s * seg_len, (s + 1) * seg_len)
            st = states[s][h]
            inter.append(_dot(qt[rows].astype(BF16), st.astype(BF16)))
            new_states[s][h] = decay_cols[ks, s:s + 1] * st + _dot_tn(
                kt[rows].astype(BF16), vf[rows].astype(BF16))
        o_h = o_h + (inter[0] if nseg == 1 else jnp.concatenate(inter, axis=0))
        outs.append(o_h)
    return jnp.concatenate(outs, axis=1), new_states


def _mixer_kernel(cfg, x_ref, mod_ref, cst_ref, gst_ref, npre_ref, npost_ref, win_ref, wal_ref,
                  convw_ref, convb_ref, walpha_ref, balpha_ref, gnorm_ref, wpa_ref, wpb_ref,
                  wout_ref, aall_ref, masks_ref, segones_ref,
                  xo_ref, cnew_ref, gnew_ref, proj_ref, o_ref, carry_ref, state_ref):
    tb, nseq, carried = cfg["tb"], cfg["nseq"], cfg["carried"]
    lseq = tb // nseq
    chunk = min(GLA_CHUNK, tb)
    seg_len = min(lseq, chunk)
    step = pl.program_id(1)

    if carried:
        @pl.when(step == 0)
        def _():
            carry_ref[...] = jnp.zeros_like(carry_ref)
            state_ref[...] = jnp.zeros_like(state_ref)

    x = x_ref[...]
    mod = mod_ref[...]
    sh = _rows_from_segments(mod[:, :, 0:D_MODEL], lseq)
    sc = _rows_from_segments(mod[:, :, D_MODEL:2 * D_MODEL], lseq)
    gate = _rows_from_segments(mod[:, :, 2 * D_MODEL:3 * D_MODEL], lseq)
    h = (_rms(x, npre_ref[...]) * (1.0 + sc) + sh).astype(BF16)
    proj_ref[...] = _dot(h, win_ref[...])
    alow = _dot(h, wal_ref[...])

    u = proj_ref[:, OFF_CG:OFF_CG + CONV_DIM] * proj_ref[:, OFF_HIN:OFF_HIN + CONV_DIM]
    prev = carry_ref[...] if carried else cst_ref[...]
    prev0 = _rows_from_segments(prev[:, 0:1, :], lseq)
    prev1 = _rows_from_segments(prev[:, 1:2, :], lseq)
    pos = lax.broadcasted_iota(jnp.int32, (tb, 1), 0) % lseq
    u1 = jnp.where(pos == 0, prev1, pltpu.roll(u, 1, 0))
    u2 = jnp.where(pos == 0, prev0, jnp.where(pos == 1, prev1, pltpu.roll(u, 2, 0)))
    cw = convw_ref[...]
    y_conv = convb_ref[...] + cw[2:3, :] * u + cw[0:1, :] * u2 + cw[1:2, :] * u1
    new_buf = u.reshape(nseq, lseq, CONV_DIM)[:, lseq - 2:lseq, :]
    if carried:
        carry_ref[...] = new_buf
    cnew_ref[...] = new_buf
    branch_a = _dot((proj_ref[:, OFF_BG:OFF_BG + CONV_DIM] * y_conv).astype(BF16), wpa_ref[...])

    z = _dot(alow.astype(BF16), walpha_ref[...]) + balpha_ref[...]
    log_a = (jnp.minimum(z, 0.0) - jnp.log1p(jnp.exp(-jnp.abs(z)))) * (1.0 / GLA_GATE_TAU)
    a_all = aall_ref[...]
    masks = [masks_ref[i] for i in range(masks_ref.shape[0])]
    segones = segones_ref[...]
    nseg = chunk // seg_len
    for c in range(tb // chunk):
        rows = slice(c * chunk, (c + 1) * chunk)
        if carried:
            states = [[state_ref[hh] for hh in range(GLA_HEADS)]]
        else:
            states = [[gst_ref[c * nseg + s, hh] for hh in range(GLA_HEADS)] for s in range(nseg)]
        o_c, new_states = _gla_chunk(
            proj_ref[rows, OFF_Q:OFF_Q + GLA_DK] * (GLA_HEAD_K ** -0.5),
            proj_ref[rows, OFF_K:OFF_K + GLA_DK],
            proj_ref[rows, OFF_V:OFF_V + GLA_DV],
            log_a[rows], a_all, masks, segones, states, seg_len)
        o_ref[rows, :] = o_c
        for s in range(nseg):
            for hh in range(GLA_HEADS):
                if carried:
                    state_ref[hh] = new_states[s][hh]
                else:
                    gnew_ref[c * nseg + s, hh] = new_states[s][hh]
    if carried:
        gnew_ref[0] = state_ref[...]

    gn = gnorm_ref[...]
    normed = []
    for hh in range(GLA_HEADS):
        vs = slice(hh * GLA_HEAD_V, (hh + 1) * GLA_HEAD_V)
        normed.append(_rms(o_ref[:, vs], gn[:, vs]))
    r = proj_ref[:, OFF_R:OFF_R + GLA_DV]
    ob = jnp.concatenate(normed, axis=1) * (r * _sigmoid(r))
    branch_b = _dot(ob.astype(BF16), wpb_ref[...])

    merged = (_sigmoid(proj_ref[:, OFF_GA:OFF_GA + D_MODEL]) * branch_a
              + _sigmoid(proj_ref[:, OFF_GB:OFF_GB + D_MODEL]) * branch_b)
    m = _dot(merged.astype(BF16), wout_ref[...])
    xo_ref[...] = x + gate * _rms(m, npost_ref[...])


def _mixer_call(x2d, mod3, conv_state, gla_state, weights, *, nseq_total, seq_len, carried):
    if carried:
        tb, nseq = MIX_TB_PROMPT, 1
        grid = (nseq_total, seq_len // tb)
        x_map = lambda b, c: (b * (seq_len // tb) + c, 0)
    else:
        nseq = MIX_SEQ_SAMPLE
        tb = nseq * seq_len
        grid = (nseq_total // nseq, 1)
        x_map = lambda b, c: (b, 0)
    chunk = min(GLA_CHUNK, tb)
    seg_len = min(tb // nseq, chunk)
    a_all, masks, segones = _gla_consts(chunk, seg_len)
    cfg = dict(tb=tb, nseq=nseq, carried=carried)
    const2 = lambda b, c: (0, 0)
    const3 = lambda b, c: (0, 0, 0)
    (npre, npost, w_in_main, w_al, conv_w, conv_b, w_alpha, b_alpha, gnorm, wpa, wpb, wout) = weights
    n_tok = x2d.shape[0]
    state_spec = pl.BlockSpec(
        (nseq, GLA_HEADS, GLA_HEAD_K, GLA_HEAD_V), lambda b, c: (b, 0, 0, 0),
        **({} if carried else dict(pipeline_mode=pl.Buffered(1))))
    in_specs = [
        pl.BlockSpec((tb, D_MODEL), x_map),
        pl.BlockSpec((nseq, 1, 3 * D_MODEL), lambda b, c: (b, 0, 0)),
        pl.BlockSpec((nseq, CONV_W - 1, CONV_DIM), lambda b, c: (b, 0, 0)),
        state_spec,
        _resident((1, D_MODEL), const2),
        _resident((1, D_MODEL), const2),
        _resident((D_MODEL, W_IN_MAIN), const2),
        _resident((D_MODEL, LANES), const2),
        _resident((CONV_W, CONV_DIM), const2),
        _resident((1, CONV_DIM), const2),
        _resident((LANES, GLA_DK), const2),
        _resident((1, GLA_DK), const2),
        _resident((1, GLA_DV), const2),
        _resident((CONV_DIM, D_MODEL), const2),
        _resident((GLA_DV, D_MODEL), const2),
        _resident((D_MODEL, D_MODEL), const2),
        _resident(a_all.shape, const2),
        _resident(masks.shape, const3),
        _resident(segones.shape, const2),
    ]
    out_specs = [
        pl.BlockSpec((tb, D_MODEL), x_map),
        pl.BlockSpec((nseq, CONV_W - 1, CONV_DIM), lambda b, c: (b, 0, 0)),
        state_spec,
    ]
    out_shape = [
        jax.ShapeDtypeStruct((n_tok, D_MODEL), F32),
        jax.ShapeDtypeStruct((nseq_total, CONV_W - 1, CONV_DIM), F32),
        jax.ShapeDtypeStruct((nseq_total, GLA_HEADS, GLA_HEAD_K, GLA_HEAD_V), F32),
    ]
    scratch = [
        pltpu.VMEM((tb, W_IN_MAIN), F32),
        pltpu.VMEM((tb, GLA_DV), F32),
        pltpu.VMEM((1, CONV_W - 1, CONV_DIM), F32),
        pltpu.VMEM((GLA_HEADS, GLA_HEAD_K, GLA_HEAD_V), F32),
    ]
    return pl.pallas_call(
        functools.partial(_mixer_kernel, cfg),
        grid=grid,
        in_specs=in_specs,
        out_specs=out_specs,
        out_shape=out_shape,
        scratch_shapes=scratch,
        compiler_params=pltpu.CompilerParams(
            dimension_semantics=("arbitrary", "arbitrary"), vmem_limit_bytes=VMEM_LIMIT_BYTES),
        name="mixer_prompt" if carried else "mixer_sample",
    )(x2d, mod3, conv_state, gla_state, npre, npost, w_in_main, w_al, conv_w, conv_b, w_alpha,
      b_alpha, gnorm, wpa, wpb, wout, a_all, masks, segones)


def _wc_kernel(sk_ref, wq_ref, o_ref):
    o_ref[0] = lax.dot_general(sk_ref[0], wq_ref[...], (((1,), (1,)), ((), ())),
                               precision=lax.Precision.HIGHEST,
                               preferred_element_type=F32).astype(BF16)


def _wc_call(subkeys, wq):
    nhp = PEER_HEADS * 2
    out = pl.pallas_call(
        _wc_kernel,
        grid=(nhp,),
        in_specs=[
            pl.BlockSpec((1, PEER_NKEYS, PEER_HALF), lambda i: (i, 0, 0)),
            pl.BlockSpec((D_MODEL, PEER_HALF), lambda i: (0, i)),
        ],
        out_specs=pl.BlockSpec((1, PEER_NKEYS, D_MODEL), lambda i: (i, 0, 0)),
        out_shape=jax.ShapeDtypeStruct((nhp, PEER_NKEYS, D_MODEL), BF16),
        compiler_params=pltpu.CompilerParams(
            dimension_semantics=("arbitrary",), vmem_limit_bytes=VMEM_LIMIT_BYTES),
        name="peer_wc",
    )(subkeys.reshape(nhp, PEER_NKEYS, PEER_HALF), wq)
    return out.reshape(PEER_HEADS, 2, PEER_NKEYS, D_MODEL).transpose(1, 0, 2, 3).reshape(
        2 * PEER_HEADS * PEER_NKEYS, D_MODEL)


def _cand_pairs():
    n = PEER_TOPK + 1
    return [(a, b) for a in range(n) for b in range(n) if (a + 1) * (b + 1) <= n]


def _desc_distinct(vals, count):
    def tree_max(xs):
        xs = list(xs)
        while len(xs) > 1:
            xs = [jnp.maximum(xs[i], xs[i + 1]) if i + 1 < len(xs) else xs[i]
                  for i in range(0, len(xs), 2)]
        return xs[0]
    out = [tree_max(vals)]
    for _ in range(count - 1):
        m = out[-1]
        out.append(tree_max([jnp.where(v < m, v, -jnp.inf) for v in vals]))
    return out


def _peer_select(st_ref, c1_ref, e1_ref, e2_ref, lg):
    nk = PEER_TOPK + 1
    tops = [[None] * PEER_HEADS for _ in range(2)]
    for p in range(2):
        for h in range(PEER_HEADS):
            r0 = (p * PEER_HEADS + h) * PEER_NKEYS
            s = st_ref[lg, r0:r0 + PEER_NKEYS, :]
            out = [jnp.max(s, axis=0, keepdims=True)]
            for _ in range(nk - 1):
                out.append(jnp.max(jnp.where(s < out[-1], s, -jnp.inf), axis=0, keepdims=True))
            tops[p][h] = out
    packed = [[jnp.concatenate([tops[p][h][k] for h in range(PEER_HEADS)], axis=0)
               for k in range(nk)] for p in range(2)]
    cands = [packed[0][a] + packed[1][b] for (a, b) in _cand_pairs()]
    t = _desc_distinct(cands, nk)
    tau = 0.5 * (t[PEER_TOPK - 1] + t[PEER_TOPK])
    z = jnp.exp(t[0] - t[0])
    for k in range(1, PEER_TOPK):
        z = z + jnp.exp(t[k] - t[0])
    half_inv_z = 0.5 / z
    for h in range(PEER_HEADS):
        s1 = st_ref[lg, h * PEER_NKEYS:(h + 1) * PEER_NKEYS, :]
        r2 = (PEER_HEADS + h) * PEER_NKEYS
        s2 = st_ref[lg, r2:r2 + PEER_NKEYS, :]
        c1_ref[lg, h] = tau[h:h + 1, :] - s1
        e1_ref[lg, h] = jnp.exp(s1 - tops[0][h][0]) * half_inv_z[h:h + 1, :]
        e2_ref[lg, h] = jnp.exp(s2 - tops[1][h][0])


def _peer_kernel(cfg, x_ref, mod_ref, npre_ref, npost_ref, wct_ref, u_ref, vt_ref, y_ref,
                 h2_ref, st_ref, c1_ref, e1_ref, e2_ref, a_ref, w_ref, acc_ref):
    tb, nseq = cfg["tb"], cfg["nseq"]
    lseq = tb // nseq
    e = pl.program_id(1)
    n_e = pl.num_programs(1)
    n_lg = tb // LANES
    n_sl = PEER_EC // PEER_SLICE
    i1_per_slice = PEER_SLICE // PEER_NKEYS

    def lane_group_rows(seg_vals, lg):
        if nseq == 1:
            return _rows_from_segments(seg_vals, LANES)
        per_lg = nseq // n_lg
        return _rows_from_segments(seg_vals[lg * per_lg:(lg + 1) * per_lg], lseq)

    def store_pre_activations(slot, u_rows):
        a_s = _dot(u_rows, h2_ref[...])
        for lg in range(n_lg):
            a_ref[slot, lg] = a_s[:, lg * LANES:(lg + 1) * LANES]

    @pl.when(e == 0)
    def _():
        mod = mod_ref[...]
        for lg in range(n_lg):
            sh = lane_group_rows(mod[:, :, 0:D_MODEL], lg)
            sc = lane_group_rows(mod[:, :, D_MODEL:2 * D_MODEL], lg)
            h2 = _rms(x_ref[lg * LANES:(lg + 1) * LANES, :], npre_ref[...]) * (1.0 + sc) + sh
            h2_ref[:, lg * LANES:(lg + 1) * LANES] = h2.T.astype(BF16)
        score_rows = 2 * PEER_NKEYS
        for r in range(0, 2 * PEER_HEADS * PEER_NKEYS, score_rows):
            st = _dot(wct_ref[r:r + score_rows, :], h2_ref[...])
            for lg in range(n_lg):
                st_ref[lg, r:r + score_rows, :] = st[:, lg * LANES:(lg + 1) * LANES]

        def body(lg, carry):
            _peer_select(st_ref, c1_ref, e1_ref, e2_ref, lg)
            return carry
        lax.fori_loop(0, n_lg, body, 0)
        acc_ref[...] = jnp.zeros_like(acc_ref)

    i1_base = pl.multiple_of(e * PEER_I1, SUBLANES)

    def gated_activations(s, lg, lanes):
        tile0 = (s * i1_per_slice) // SUBLANES * SUBLANES
        i1_tile = pl.ds(pl.multiple_of(i1_base + tile0, SUBLANES), SUBLANES)
        c1t = [c1_ref[lg, h, i1_tile, :] for h in range(PEER_HEADS)]
        e1t = [e1_ref[lg, h, i1_tile, :] for h in range(PEER_HEADS)]
        for i in range(i1_per_slice):
            r = s * i1_per_slice + i - tile0
            g = jnp.zeros((PEER_NKEYS, LANES), F32)
            for h in range(PEER_HEADS):
                r2 = (PEER_HEADS + h) * PEER_NKEYS
                s2 = st_ref[lg, r2:r2 + PEER_NKEYS, :]
                p = e2_ref[lg, h] * e1t[h][r:r + 1, :]
                g = g + jnp.where(s2 >= c1t[h][r:r + 1, :], p, 0.0)
            rows = slice(i * PEER_NKEYS, (i + 1) * PEER_NKEYS)
            a = a_ref[s % 2, lg, rows, :]
            act2 = a + a * jnp.tanh(GELU_C0 * (a + GELU_C1 * (a * a * a)))
            w_ref[s % 2, rows, lanes] = (act2 * g).astype(BF16)

    def accumulate(s):
        acc_ref[...] += _dot(vt_ref[:, s * PEER_SLICE:(s + 1) * PEER_SLICE], w_ref[s % 2])

    store_pre_activations(0, u_ref[0:PEER_SLICE, :])
    for s in range(n_sl):
        if s + 1 < n_sl:
            store_pre_activations((s + 1) % 2, u_ref[(s + 1) * PEER_SLICE:(s + 2) * PEER_SLICE, :])
        for lg in range(n_lg):
            gated_activations(s, lg, slice(lg * LANES, (lg + 1) * LANES))
        if s >= 1:
            accumulate(s - 1)
    accumulate(n_sl - 1)

    @pl.when(e == n_e - 1)
    def _():
        mod = mod_ref[...]
        for lg in range(n_lg):
            rows = slice(lg * LANES, (lg + 1) * LANES)
            gate = lane_group_rows(mod[:, :, 2 * D_MODEL:3 * D_MODEL], lg)
            out = acc_ref[:, rows].T
            y_ref[rows, :] = x_ref[rows, :] + gate * _rms(out, npost_ref[...])


def _peer_call(x2d, mod3, npre, npost, wct, u_bf, vt_bf, *, nseq_total, seq_len):
    n_tok = x2d.shape[0]
    tb = PEER_TB
    if seq_len >= tb:
        nseq = 1
        mod_map = lambda i, e: (i // (seq_len // tb), 0, 1)
    else:
        nseq = tb // seq_len
        mod_map = lambda i, e: (i, 0, 1)
    n_exp = u_bf.shape[0]
    n_sl = PEER_EC // PEER_SLICE
    n_slices = n_exp // PEER_SLICE
    assert n_sl % 2 == 0 and n_exp % PEER_EC == 0
    cfg = dict(tb=tb, nseq=nseq)
    const2 = lambda i, e: (0, 0)
    return pl.pallas_call(
        functools.partial(_peer_kernel, cfg),
        grid=(n_tok // tb, n_exp // PEER_EC),
        in_specs=[
            pl.BlockSpec((tb, D_MODEL), lambda i, e: (i, 0), pipeline_mode=pl.Buffered(1)),
            pl.BlockSpec((nseq, 1, 3 * D_MODEL), mod_map, pipeline_mode=pl.Buffered(1)),
            _resident((1, D_MODEL), const2),
            _resident((1, D_MODEL), const2),
            _resident(wct.shape, const2),
            pl.BlockSpec((PEER_EC, D_MODEL), lambda i, e: (e, 0)),
            pl.BlockSpec((D_MODEL, PEER_EC), lambda i, e: (0, e)),
        ],
        out_specs=pl.BlockSpec((tb, D_MODEL), lambda i, e: (i, 0)),
        out_shape=jax.ShapeDtypeStruct((n_tok, D_MODEL), F32),
        scratch_shapes=[
            pltpu.VMEM((D_MODEL, tb), BF16),
            pltpu.VMEM((tb // LANES, 2 * PEER_HEADS * PEER_NKEYS, LANES), F32),
            pltpu.VMEM((tb // LANES, PEER_HEADS, PEER_NKEYS, LANES), F32),
            pltpu.VMEM((tb // LANES, PEER_HEADS, PEER_NKEYS, LANES), F32),
            pltpu.VMEM((tb // LANES, PEER_HEADS, PEER_NKEYS, LANES), F32),
            pltpu.VMEM((2, tb // LANES, PEER_SLICE, LANES), F32),
            pltpu.VMEM((2, PEER_SLICE, tb), BF16),
            pltpu.VMEM((D_MODEL, tb), F32),
        ],
        compiler_params=pltpu.CompilerParams(
            dimension_semantics=("arbitrary", "arbitrary"), vmem_limit_bytes=VMEM_LIMIT_BYTES),
        name="peer_prompt" if nseq == 1 else "peer_sample",
    )(x2d, mod3, npre, npost, wct, u_bf, vt_bf)


def kernel(x_prompt, x_sample, c_prompt, c_sample, state_conv, state_gla, norm_mix_pre, norm_mix_post, norm_ffn_pre, norm_ffn_post, w_ada, b_ada, w_in, conv_w, conv_b, w_alpha, b_alpha, gla_norm_w, w_proj_a, w_proj_b, w_out, peer_wq, peer_subkeys, peer_u, peer_v):
    depth = w_in.shape[0]
    assert depth == 1, "single-layer trunk"
    n_p, t_p, _ = x_prompt.shape
    n_s, t_s, _ = x_sample.shape
    l = 0

    c_all = jnp.concatenate([c_prompt, c_sample], axis=0)
    mod = _ada_call(c_all, w_ada[l], b_ada[l])
    mod_p = mod[:n_p].reshape(n_p, 1, N_MOD * D_MODEL)
    mod_s = mod[n_p:].reshape(n_s, 1, N_MOD * D_MODEL)

    w_in_main = w_in[l][:, :W_IN_MAIN].astype(BF16)
    w_al = jnp.pad(w_in[l][:, W_IN_MAIN:], ((0, 0), (0, LANES - GLA_GATE_RANK))).astype(BF16)
    w_alpha_p = jnp.pad(w_alpha[l], ((0, LANES - GLA_GATE_RANK), (0, 0))).astype(BF16)
    row = lambda a: a.reshape(1, -1)
    mixer_w = (row(norm_mix_pre[l]), row(norm_mix_post[l]), w_in_main, w_al, conv_w[l], row(conv_b[l]),
               w_alpha_p, row(b_alpha[l]), row(gla_norm_w[l]), w_proj_a[l].astype(BF16),
               w_proj_b[l].astype(BF16), w_out[l].astype(BF16))

    zeros_conv = jnp.zeros((n_p, CONV_W - 1, CONV_DIM), state_conv.dtype)
    zeros_gla = jnp.zeros((n_p, GLA_HEADS, GLA_HEAD_K, GLA_HEAD_V), state_gla.dtype)
    x1_p, conv_p, gla_p = _mixer_call(
        x_prompt.reshape(n_p * t_p, D_MODEL), mod_p, zeros_conv, zeros_gla, mixer_w,
        nseq_total=n_p, seq_len=t_p, carried=True)
    x1_s, conv_s, gla_s = _mixer_call(
        x_sample.reshape(n_s * t_s, D_MODEL), mod_s, state_conv[l], state_gla[l], mixer_w,
        nseq_total=n_s, seq_len=t_s, carried=False)

    wct = _wc_call(peer_subkeys[l], peer_wq[l])
    u_bf = peer_u[l].astype(BF16)
    vt_bf = peer_v[l].astype(BF16).T
    npre2, npost2 = row(norm_ffn_pre[l]), row(norm_ffn_post[l])
    y_p = _peer_call(x1_p, mod_p, npre2, npost2, wct, u_bf, vt_bf, nseq_total=n_p, seq_len=t_p)
    y_s = _peer_call(x1_s, mod_s, npre2, npost2, wct, u_bf, vt_bf, nseq_total=n_s, seq_len=t_s)

    return (y_p.reshape(n_p, t_p, D_MODEL), y_s.reshape(n_s, t_s, D_MODEL),
            conv_p[None], gla_p[None], conv_s[None], gla_s[None])
```

```python
import functools
import math

import jax
import jax.numpy as jnp
import numpy as np
from jax import lax
from jax.experimental import pallas as pl
from jax.experimental.pallas import tpu as pltpu

F32 = jnp.float32
BF16 = jnp.bfloat16

D_MODEL = 1024
CONV_DIM = 512
CONV_W = 3
GLA_HEADS = 4
GLA_DK = 512
GLA_DV = 1024
GLA_HEAD_K = 128
GLA_HEAD_V = 256
GLA_GATE_RANK = 16
GLA_GATE_TAU = 16.0
PEER_HEADS = 8
PEER_NKEYS = 128
PEER_HALF = 128
PEER_TOPK = 16
N_MOD = 6
EPS = 1e-6
W_IN_MAIN = 3 * CONV_DIM + 2 * GLA_DK + 2 * GLA_DV + 2 * D_MODEL
OFF_BG, OFF_CG, OFF_HIN = 0, 512, 1024
OFF_Q, OFF_K, OFF_V, OFF_R, OFF_GA, OFF_GB = 1536, 2048, 2560, 3584, 4608, 5632

LANES = 128
SUBLANES = 8
VMEM_LIMIT_BYTES = 58 * 1024 * 1024

GLA_CHUNK = 128
MIX_TB_PROMPT = 256
MIX_SEQ_SAMPLE = 16
PEER_TB = 512
PEER_EC = 2048
PEER_I1 = PEER_EC // PEER_NKEYS
GELU_C0 = math.sqrt(2.0 / math.pi)
GELU_C1 = 0.044715


def _dot(a, b):
    return jnp.dot(a, b, preferred_element_type=F32)


def _dot_nt(a, b):
    return lax.dot_general(a, b, (((1,), (1,)), ((), ())), preferred_element_type=F32)


def _dot_tn(a, b):
    return lax.dot_general(a, b, (((0,), (0,)), ((), ())), preferred_element_type=F32)


def _split3(x):
    p0 = x.astype(BF16)
    r = x - p0.astype(F32)
    p1 = r.astype(BF16)
    r = r - p1.astype(F32)
    return p0, p1, r.astype(BF16)


def _rms(x, w):
    return x * lax.rsqrt(jnp.mean(x * x, axis=-1, keepdims=True) + EPS) * w


def _sigmoid(x):
    return 1.0 / (1.0 + jnp.exp(-x))


def _resident(shape, index_map):
    return pl.BlockSpec(shape, index_map, pipeline_mode=pl.Buffered(1))


def _rows_from_segments(seg_vals, seg_len):
    nseg, _, d = seg_vals.shape
    return jnp.broadcast_to(seg_vals, (nseg, seg_len, d)).reshape(nseg * seg_len, d)


def _ada_kernel(c_ref, w_ref, b_ref, o_ref):
    c = c_ref[...]
    a = (c * _sigmoid(c)).astype(BF16)
    o_ref[...] = _dot(a, w_ref[...].astype(BF16)) + b_ref[...]


def _ada_call(c_all, w_ada, b_ada):
    rows = c_all.shape[0]
    ncols = w_ada.shape[1]
    bn = 1536
    return pl.pallas_call(
        _ada_kernel,
        grid=(ncols // bn,),
        in_specs=[
            pl.BlockSpec((rows, D_MODEL), lambda j: (0, 0)),
            pl.BlockSpec((D_MODEL, bn), lambda j: (0, j)),
            pl.BlockSpec((1, bn), lambda j: (0, j)),
        ],
        out_specs=pl.BlockSpec((rows, bn), lambda j: (0, j)),
        out_shape=jax.ShapeDtypeStruct((rows, ncols), F32),
        compiler_params=pltpu.CompilerParams(
            dimension_semantics=("arbitrary",), vmem_limit_bytes=VMEM_LIMIT_BYTES),
        name="ada_mod",
    )(c_all, w_ada, b_ada.reshape(1, ncols))


def _gla_levels(seg_len):
    return [b for b in (1, 2, 4, 8, 16, 32, 64) if 2 * b <= seg_len]


def _gla_consts(chunk, seg_len):
    t = np.arange(chunk)
    seg = t // seg_len
    blocks = [(t[None, :] <= t[:, None]) & (seg[None, :] == seg[:, None])]
    masks = []
    for b in _gla_levels(seg_len):
        parent = t // (2 * b)
        refpos = parent * (2 * b) + b - 1
        upper = (t % (2 * b)) >= b
        a = np.zeros((chunk, chunk), bool)
        for i in range(chunk):
            if upper[i]:
                a[i, refpos[i] + 1:i + 1] = True
            else:
                a[i, i + 1:refpos[i] + 1] = True
        blocks.append(a)
        masks.append((parent[:, None] == parent[None, :]) & upper[:, None] & ~upper[None, :])
    segend = (seg + 1) * seg_len - 1
    a = np.zeros((chunk, chunk), bool)
    for i in range(chunk):
        a[i, i + 1:segend[i] + 1] = True
    blocks.append(a)
    masks.append(np.eye(chunk, dtype=bool))
    segones = np.zeros((chunk, LANES), np.float32)
    segones[t, seg] = 1.0
    return (jnp.asarray(np.concatenate(blocks, 0).astype(np.float32), BF16),
            jnp.asarray(np.stack(masks).astype(np.float32)),
            jnp.asarray(segones, BF16))


def _gla_chunk(q, k, v, la, a_all, masks, segones, states, seg_len):
    c = q.shape[0]
    nseg = c // seg_len
    nl = len(_gla_levels(seg_len))
    pieces = _split3(la)
    d_all = sum(_dot(a_all, p) for p in pieces)
    e_all = jnp.exp(d_all)
    cum_last_t = sum(_dot_tn(p, segones) for p in pieces)
    decay_cols = jnp.exp(cum_last_t)
    e_cum = e_all[0:c]
    e_last = e_all[(nl + 1) * c:(nl + 2) * c]
    outs = []
    new_states = [[None] * GLA_HEADS for _ in range(nseg)]
    for h in range(GLA_HEADS):
        ks = slice(h * GLA_HEAD_K, (h + 1) * GLA_HEAD_K)
        vs = slice(h * GLA_HEAD_V, (h + 1) * GLA_HEAD_V)
        qh, kh = q[:, ks], k[:, ks]
        vh = v[:, vs].astype(BF16)
        scores = _dot_nt(qh.astype(BF16), kh.astype(BF16)) * masks[nl]
        for li in range(nl):
            eb = e_all[(1 + li) * c:(2 + li) * c, ks]
            scores = scores + _dot_nt((qh * eb).astype(BF16), (kh * eb).astype(BF16)) * masks[li]
        o_h = _dot(scores.astype(BF16), vh)
        qt = qh * e_cum[:, ks]
        kt = kh * e_last[:, ks]
        vf = v[:, vs]
        inter = []
        for s in range(nseg):
            rows = slice(s * seg_len, (s + 1) * seg_len)
            st = states[s][h]
            inter.append(_dot(qt[rows].astype(BF16), st.astype(BF16)))
            new_states[s][h] = decay_cols[ks, s:s + 1] * st + _dot_tn(
                kt[rows].astype(BF16), vf[rows].astype(BF16))
        o_h = o_h + (inter[0] if nseg == 1 else jnp.concatenate(inter, axis=0))
        outs.append(o_h)
    return jnp.concatenate(outs, axis=1), new_states


def _mixer_kernel(cfg, x_ref, mod_ref, cst_ref, gst_ref, npre_ref, npost_ref, win_ref, wal_ref,
                  convw_ref, convb_ref, walpha_ref, balpha_ref, gnorm_ref, wpa_ref, wpb_ref,
                  wout_ref, aall_ref, masks_ref, segones_ref,
                  xo_ref, cnew_ref, gnew_ref, proj_ref, o_ref, carry_ref, state_ref):
    tb, nseq, carried = cfg["tb"], cfg["nseq"], cfg["carried"]
    lseq = tb // nseq
    chunk = min(GLA_CHUNK, tb)
    seg_len = min(lseq, chunk)
    step = pl.program_id(1)

    if carried:
        @pl.when(step == 0)
        def _():
            carry_ref[...] = jnp.zeros_like(carry_ref)
            state_ref[...] = jnp.zeros_like(state_ref)

    x = x_ref[...]
    mod = mod_ref[...]
    sh = _rows_from_segments(mod[:, :, 0:D_MODEL], lseq)
    sc = _rows_from_segments(mod[:, :, D_MODEL:2 * D_MODEL], lseq)
    gate = _rows_from_segments(mod[:, :, 2 * D_MODEL:3 * D_MODEL], lseq)
    h = (_rms(x, npre_ref[...]) * (1.0 + sc) + sh).astype(BF16)
    proj_ref[...] = _dot(h, win_ref[...])
    alow = _dot(h, wal_ref[...])

    u = proj_ref[:, OFF_CG:OFF_CG + CONV_DIM] * proj_ref[:, OFF_HIN:OFF_HIN + CONV_DIM]
    prev = carry_ref[...] if carried else cst_ref[...]
    prev0 = _rows_from_segments(prev[:, 0:1, :], lseq)
    prev1 = _rows_from_segments(prev[:, 1:2, :], lseq)
    pos = lax.broadcasted_iota(jnp.int32, (tb, 1), 0) % lseq
    u1 = jnp.where(pos == 0, prev1, pltpu.roll(u, 1, 0))
    u2 = jnp.where(pos == 0, prev0, jnp.where(pos == 1, prev1, pltpu.roll(u, 2, 0)))
    cw = convw_ref[...]
    y_conv = convb_ref[...] + cw[2:3, :] * u + cw[0:1, :] * u2 + cw[1:2, :] * u1
    new_buf = u.reshape(nseq, lseq, CONV_DIM)[:, lseq - 2:lseq, :]
    if carried:
        carry_ref[...] = new_buf
    cnew_ref[...] = new_buf
    branch_a = _dot((proj_ref[:, OFF_BG:OFF_BG + CONV_DIM] * y_conv).astype(BF16), wpa_ref[...])

    z = _dot(alow.astype(BF16), walpha_ref[...]) + balpha_ref[...]
    log_a = (jnp.minimum(z, 0.0) - jnp.log1p(jnp.exp(-jnp.abs(z)))) * (1.0 / GLA_GATE_TAU)
    a_all = aall_ref[...]
    masks = [masks_ref[i] for i in range(masks_ref.shape[0])]
    segones = segones_ref[...]
    nseg = chunk // seg_len
    for c in range(tb // chunk):
        rows = slice(c * chunk, (c + 1) * chunk)
        if carried:
            states = [[state_ref[hh] for hh in range(GLA_HEADS)]]
        else:
            states = [[gst_ref[c * nseg + s, hh] for hh in range(GLA_HEADS)] for s in range(nseg)]
        o_c, new_states = _gla_chunk(
            proj_ref[rows, OFF_Q:OFF_Q + GLA_DK] * (GLA_HEAD_K ** -0.5),
            proj_ref[rows, OFF_K:OFF_K + GLA_DK],
            proj_ref[rows, OFF_V:OFF_V + GLA_DV],
            log_a[rows], a_all, masks, segones, states, seg_len)
        o_ref[rows, :] = o_c
        for s in range(nseg):
            for hh in range(GLA_HEADS):
                if carried:
                    state_ref[hh] = new_states[s][hh]
                else:
                    gnew_ref[c * nseg + s, hh] = new_states[s][hh]
    if carried:
        gnew_ref[0] = state_ref[...]

    gn = gnorm_ref[...]
    normed = []
    for hh in range(GLA_HEADS):
        vs = slice(hh * GLA_HEAD_V, (hh + 1) * GLA_HEAD_V)
        normed.append(_rms(o_ref[:, vs], gn[:, vs]))
    r = proj_ref[:, OFF_R:OFF_R + GLA_DV]
    ob = jnp.concatenate(normed, axis=1) * (r * _sigmoid(r))
    branch_b = _dot(ob.astype(BF16), wpb_ref[...])

    merged = (_sigmoid(proj_ref[:, OFF_GA:OFF_GA + D_MODEL]) * branch_a
              + _sigmoid(proj_ref[:, OFF_GB:OFF_GB + D_MODEL]) * branch_b)
    m = _dot(merged.astype(BF16), wout_ref[...])
    xo_ref[...] = x + gate * _rms(m, npost_ref[...])


def _mixer_call(x2d, mod3, conv_state, gla_state, weights, *, nseq_total, seq_len, carried):
    if carried:
        tb, nseq = MIX_TB_PROMPT, 1
        grid = (nseq_total, seq_len // tb)
        x_map = lambda b, c: (b * (seq_len // tb) + c, 0)
    else:
        nseq = MIX_SEQ_SAMPLE
        tb = nseq * seq_len
        grid = (nseq_total // nseq, 1)
        x_map = lambda b, c: (b, 0)
    chunk = min(GLA_CHUNK, tb)
    seg_len = min(tb // nseq, chunk)
    a_all, masks, segones = _gla_consts(chunk, seg_len)
    cfg = dict(tb=tb, nseq=nseq, carried=carried)
    const2 = lambda b, c: (0, 0)
    const3 = lambda b, c: (0, 0, 0)
    (npre, npost, w_in_main, w_al, conv_w, conv_b, w_alpha, b_alpha, gnorm, wpa, wpb, wout) = weights
    n_tok = x2d.shape[0]
    state_spec = pl.BlockSpec(
        (nseq, GLA_HEADS, GLA_HEAD_K, GLA_HEAD_V), lambda b, c: (b, 0, 0, 0),
        **({} if carried else dict(pipeline_mode=pl.Buffered(1))))
    in_specs = [
        pl.BlockSpec((tb, D_MODEL), x_map),
        pl.BlockSpec((nseq, 1, 3 * D_MODEL), lambda b, c: (b, 0, 0)),
        pl.BlockSpec((nseq, CONV_W - 1, CONV_DIM), lambda b, c: (b, 0, 0)),
        state_spec,
        _resident((1, D_MODEL), const2),
        _resident((1, D_MODEL), const2),
        _resident((D_MODEL, W_IN_MAIN), const2),
        _resident((D_MODEL, LANES), const2),
        _resident((CONV_W, CONV_DIM), const2),
        _resident((1, CONV_DIM), const2),
        _resident((LANES, GLA_DK), const2),
        _resident((1, GLA_DK), const2),
        _resident((1, GLA_DV), const2),
        _resident((CONV_DIM, D_MODEL), const2),
        _resident((GLA_DV, D_MODEL), const2),
        _resident((D_MODEL, D_MODEL), const2),
        _resident(a_all.shape, const2),
        _resident(masks.shape, const3),
        _resident(segones.shape, const2),
    ]
    out_specs = [
        pl.BlockSpec((tb, D_MODEL), x_map),
        pl.BlockSpec((nseq, CONV_W - 1, CONV_DIM), lambda b, c: (b, 0, 0)),
        state_spec,
    ]
    out_shape = [
        jax.ShapeDtypeStruct((n_tok, D_MODEL), F32),
        jax.ShapeDtypeStruct((nseq_total, CONV_W - 1, CONV_DIM), F32),
        jax.ShapeDtypeStruct((nseq_total, GLA_HEADS, GLA_HEAD_K, GLA_HEAD_V), F32),
    ]
    scratch = [
        pltpu.VMEM((tb, W_IN_MAIN), F32),
        pltpu.VMEM((tb, GLA_DV), F32),
        pltpu.VMEM((1, CONV_W - 1, CONV_DIM), F32),
        pltpu.VMEM((GLA_HEADS, GLA_HEAD_K, GLA_HEAD_V), F32),
    ]
    return pl.pallas_call(
        functools.partial(_mixer_kernel, cfg),
        grid=grid,
        in_specs=in_specs,
        out_specs=out_specs,
        out_shape=out_shape,
        scratch_shapes=scratch,
        compiler_params=pltpu.CompilerParams(
            dimension_semantics=("arbitrary", "arbitrary"), vmem_limit_bytes=VMEM_LIMIT_BYTES),
        name="mixer_prompt" if carried else "mixer_sample",
    )(x2d, mod3, conv_state, gla_state, npre, npost, w_in_main, w_al, conv_w, conv_b, w_alpha,
      b_alpha, gnorm, wpa, wpb, wout, a_all, masks, segones)


def _wc_kernel(sk_ref, wq_ref, o_ref):
    o_ref[0] = lax.dot_general(sk_ref[0], wq_ref[...], (((1,), (1,)), ((), ())),
                               precision=lax.Precision.HIGHEST,
                               preferred_element_type=F32).astype(BF16)


def _wc_call(subkeys, wq):
    nhp = PEER_HEADS * 2
    out = pl.pallas_call(
        _wc_kernel,
        grid=(nhp,),
        in_specs=[
            pl.BlockSpec((1, PEER_NKEYS, PEER_HALF), lambda i: (i, 0, 0)),
            pl.BlockSpec((D_MODEL, PEER_HALF), lambda i: (0, i)),
        ],
        out_specs=pl.BlockSpec((1, PEER_NKEYS, D_MODEL), lambda i: (i, 0, 0)),
        out_shape=jax.ShapeDtypeStruct((nhp, PEER_NKEYS, D_MODEL), BF16),
        compiler_params=pltpu.CompilerParams(
            dimension_semantics=("arbitrary",), vmem_limit_bytes=VMEM_LIMIT_BYTES),
        name="peer_wc",
    )(subkeys.reshape(nhp, PEER_NKEYS, PEER_HALF), wq)
    return out.reshape(PEER_HEADS, 2, PEER_NKEYS, D_MODEL).transpose(1, 0, 2, 3).reshape(
        2 * PEER_HEADS * PEER_NKEYS, D_MODEL)


def _cand_pairs():
    n = PEER_TOPK + 1
    return [(a, b) for a in range(n) for b in range(n) if (a + 1) * (b + 1) <= n]


def _tree_reduce(op, xs):
    xs = list(xs)
    while len(xs) > 1:
        xs = [op(xs[i], xs[i + 1]) if i + 1 < len(xs) else xs[i] for i in range(0, len(xs), 2)]
    return xs[0]


def _desc_distinct(vals, count):
    out = [_tree_reduce(jnp.maximum, vals)]
    for _ in range(count - 1):
        m = out[-1]
        out.append(_tree_reduce(jnp.maximum, [jnp.where(v < m, v, -jnp.inf) for v in vals]))
    return out


def _store_gate_factors(st_ref, e1_ref, e2_ref, lg, h, max1, max2, half_inv_z_row):
    s1 = st_ref[lg, h * PEER_NKEYS:(h + 1) * PEER_NKEYS, :]
    row2 = (PEER_HEADS + h) * PEER_NKEYS
    s2 = st_ref[lg, row2:row2 + PEER_NKEYS, :]
    e1_ref[lg, h] = jnp.exp(s1 - max1) * half_inv_z_row
    e2_ref[lg, h] = jnp.exp(s2 - max2)


def _peer_select(st_ref, c1_ref, e1_ref, e2_ref, lg):
    nk = PEER_TOPK + 1
    tops = [[None] * PEER_HEADS for _ in range(2)]
    bad = jnp.zeros((1, LANES), F32)
    for p in range(2):
        for h in range(PEER_HEADS):
            r0 = (p * PEER_HEADS + h) * PEER_NKEYS
            s = st_ref[lg, r0:r0 + PEER_NKEYS, :]
            out = [jnp.max(s, axis=0, keepdims=True)]
            for _ in range(nk - 1):
                out.append(jnp.max(jnp.where(s < out[-1], s, -jnp.inf), axis=0, keepdims=True))
            tops[p][h] = out
            n_top = jnp.sum(jnp.where(s >= out[-1], 1.0, 0.0), axis=0, keepdims=True)
            bad = jnp.maximum(bad, jnp.where(n_top != float(nk), 1.0, 0.0))
    packed = [[jnp.concatenate([tops[p][h][k] for h in range(PEER_HEADS)], axis=0)
               for k in range(nk)] for p in range(2)]
    cands = [packed[0][a] + packed[1][b] for (a, b) in _cand_pairs()]
    t = _desc_distinct(cands, nk)
    n_top = _tree_reduce(jnp.add, [jnp.where(c >= t[-1], 1.0, 0.0) for c in cands])
    bad = jnp.maximum(bad, jnp.max(jnp.where(n_top != float(nk), 1.0, 0.0), axis=0, keepdims=True))
    tau = 0.5 * (t[PEER_TOPK - 1] + t[PEER_TOPK])
    z = jnp.ones_like(t[0])
    for k in range(1, PEER_TOPK):
        z = z + jnp.exp(t[k] - t[0])
    half_inv_z = 0.5 / z
    for h in range(PEER_HEADS):
        s1 = st_ref[lg, h * PEER_NKEYS:(h + 1) * PEER_NKEYS, :]
        c1_ref[lg, h] = tau[h:h + 1, :] - s1
        _store_gate_factors(st_ref, e1_ref, e2_ref, lg, h, tops[0][h][0], tops[1][h][0],
                            half_inv_z[h:h + 1, :])

    @pl.when(jnp.max(bad) > 0.0)
    def _():
        for h in range(PEER_HEADS):
            row2 = (PEER_HEADS + h) * PEER_NKEYS
            s2 = st_ref[lg, row2:row2 + PEER_NKEYS, :]
            c1 = c1_ref[lg, h]
            n_sel = jnp.zeros((PEER_NKEYS, LANES), F32)
            for k in range(nk):
                v = tops[1][h][k]
                mult = jnp.sum(jnp.where(s2 == v, 1.0, 0.0), axis=0, keepdims=True)
                n_sel = n_sel + jnp.where(v >= c1, mult, 0.0)
            total = jnp.sum(n_sel, axis=0, keepdims=True)

            @pl.when(jnp.max(jnp.where(total != float(PEER_TOPK), 1.0, 0.0)) > 0.0)
            def _(h=h):
                _peer_select_exact(st_ref, c1_ref, e1_ref, e2_ref, lg, h)


def _peer_select_exact(st_ref, c1_ref, e1_ref, e2_ref, lg, h):
    n_pairs = PEER_TOPK * PEER_TOPK
    row1 = h * PEER_NKEYS
    row2 = (PEER_HEADS + h) * PEER_NKEYS
    s1 = st_ref[lg, row1:row1 + PEER_NKEYS, :]
    s2 = st_ref[lg, row2:row2 + PEER_NKEYS, :]
    key_id = lax.broadcasted_iota(jnp.int32, (1, PEER_NKEYS, LANES), 1).astype(F32)
    slot_id = lax.broadcasted_iota(jnp.int32, (1, PEER_TOPK, 1), 1)

    def pick(k, state):
        rem, rank, vals = state
        m = jnp.max(rem, axis=1, keepdims=True)
        first = jnp.min(jnp.where(rem == m, key_id, float(PEER_NKEYS)), axis=1, keepdims=True)
        hit = key_id == first
        return (jnp.where(hit, -jnp.inf, rem), jnp.where(hit, k.astype(F32), rank),
                jnp.where(slot_id == k, m, vals))
    _, rank, vals = lax.fori_loop(
        0, PEER_TOPK, pick,
        (jnp.stack([s1, s2]), jnp.full((2, PEER_NKEYS, LANES), float(PEER_TOPK), F32),
         jnp.zeros((2, PEER_TOPK, LANES), F32)))

    cand = (vals[0][:, None, :] + vals[1][None, :, :]).reshape(n_pairs, LANES)
    flat = lax.broadcasted_iota(jnp.int32, (n_pairs, 1), 0).astype(F32)

    def pick_pair(k, state):
        rem, taken, z, t0 = state
        m = jnp.max(rem, axis=0, keepdims=True)
        first = jnp.min(jnp.where(rem == m, flat, float(n_pairs)), axis=0, keepdims=True)
        hit = flat == first
        taken = taken + jnp.sum(jnp.where(hit, 1.0, 0.0).reshape(PEER_TOPK, PEER_TOPK, LANES), axis=1)
        t0 = jnp.where(k == 0, m, t0)
        return jnp.where(hit, -jnp.inf, rem), taken, z + jnp.exp(m - t0), t0
    zeros = jnp.zeros((1, LANES), F32)
    _, taken, z, _ = lax.fori_loop(0, PEER_TOPK, pick_pair,
                                   (cand, jnp.zeros((PEER_TOPK, LANES), F32), zeros, zeros))
    n1 = jnp.zeros((PEER_NKEYS, LANES), F32)
    for a in range(PEER_TOPK):
        n1 = n1 + jnp.where(rank[0] == float(a), taken[a:a + 1, :], 0.0)
    e1_ref[lg, h] = jnp.exp(s1 - vals[0][0:1, :]) * (0.5 / z)
    e2_ref[lg, h] = jnp.exp(s2 - vals[1][0:1, :])
    c1_ref[lg, h] = 0.5 - n1
    st_ref[lg, row2:row2 + PEER_NKEYS, :] = -rank[1]


def _peer_kernel(cfg, x_ref, mod_ref, npre_ref, npost_ref, wct_ref, u_ref, vt_ref, y_ref,
                 h2_ref, st_ref, c1_ref, e1_ref, e2_ref, a_ref, w_ref, acc_ref):
    tb, nseq = cfg["tb"], cfg["nseq"]
    lseq = tb // nseq
    e = pl.program_id(1)
    n_e = pl.num_programs(1)
    n_lg = tb // LANES

    def lane_group_rows(seg_vals, lg):
        if nseq == 1:
            return _rows_from_segments(seg_vals, LANES)
        per_lg = nseq // n_lg
        return _rows_from_segments(seg_vals[lg * per_lg:(lg + 1) * per_lg], lseq)

    @pl.when(e == 0)
    def _():
        mod = mod_ref[...]
        for lg in range(n_lg):
            sh = lane_group_rows(mod[:, :, 0:D_MODEL], lg)
            sc = lane_group_rows(mod[:, :, D_MODEL:2 * D_MODEL], lg)
            h2 = _rms(x_ref[lg * LANES:(lg + 1) * LANES, :], npre_ref[...]) * (1.0 + sc) + sh
            h2_ref[:, lg * LANES:(lg + 1) * LANES] = h2.T.astype(BF16)
        score_rows = 2 * PEER_NKEYS
        for r in range(0, 2 * PEER_HEADS * PEER_NKEYS, score_rows):
            st = _dot(wct_ref[r:r + score_rows, :], h2_ref[...])
            for lg in range(n_lg):
                st_ref[lg, r:r + score_rows, :] = st[:, lg * LANES:(lg + 1) * LANES]

        def select(lg, carry):
            _peer_select(st_ref, c1_ref, e1_ref, e2_ref, lg)
            return carry
        lax.fori_loop(0, n_lg, select, 0)
        acc_ref[...] = jnp.zeros_like(acc_ref)

    a_all = _dot(u_ref[...], h2_ref[...])
    for lg in range(n_lg):
        a_ref[lg] = a_all[:, lg * LANES:(lg + 1) * LANES]
    i1_tile = pl.ds(pl.multiple_of(e * PEER_I1, SUBLANES), PEER_I1)

    def gate_lane_group(lg, carry):
        lanes = pl.ds(pl.multiple_of(lg * LANES, LANES), LANES)
        c1t = [c1_ref[lg, h, i1_tile, :] for h in range(PEER_HEADS)]
        e1t = [e1_ref[lg, h, i1_tile, :] for h in range(PEER_HEADS)]
        for i1 in range(PEER_I1):
            g = jnp.zeros((PEER_NKEYS, LANES), F32)
            for h in range(PEER_HEADS):
                row2 = (PEER_HEADS + h) * PEER_NKEYS
                s2 = st_ref[lg, row2:row2 + PEER_NKEYS, :]
                p = e2_ref[lg, h] * e1t[h][i1:i1 + 1, :]
                g = g + jnp.where(s2 >= c1t[h][i1:i1 + 1, :], p, 0.0)
            rows = slice(i1 * PEER_NKEYS, (i1 + 1) * PEER_NKEYS)
            a = a_ref[lg, rows, :]
            act2 = a + a * jnp.tanh(a * (GELU_C0 + (GELU_C0 * GELU_C1) * (a * a)))
            w_ref[rows, lanes] = (act2 * g).astype(BF16)
        return carry
    lax.fori_loop(0, n_lg, gate_lane_group, 0)
    acc_ref[...] += _dot(vt_ref[...], w_ref[...])

    @pl.when(e == n_e - 1)
    def _():
        mod = mod_ref[...]
        for lg in range(n_lg):
            rows = slice(lg * LANES, (lg + 1) * LANES)
            gate = lane_group_rows(mod[:, :, 2 * D_MODEL:3 * D_MODEL], lg)
            out = acc_ref[:, rows].T
            y_ref[rows, :] = x_ref[rows, :] + gate * _rms(out, npost_ref[...])


def _peer_call(x2d, mod3, npre, npost, wct, u_bf, vt_bf, *, nseq_total, seq_len):
    n_tok = x2d.shape[0]
    tb = PEER_TB
    if seq_len >= tb:
        nseq = 1
        mod_map = lambda i, e: (i // (seq_len // tb), 0, 1)
    else:
        nseq = tb // seq_len
        mod_map = lambda i, e: (i, 0, 1)
    n_exp = u_bf.shape[0]
    assert n_exp % PEER_EC == 0 and PEER_I1 % SUBLANES == 0
    cfg = dict(tb=tb, nseq=nseq)
    const2 = lambda i, e: (0, 0)
    return pl.pallas_call(
        functools.partial(_peer_kernel, cfg),
        grid=(n_tok // tb, n_exp // PEER_EC),
        in_specs=[
            pl.BlockSpec((tb, D_MODEL), lambda i, e: (i, 0), pipeline_mode=pl.Buffered(1)),
            pl.BlockSpec((nseq, 1, 3 * D_MODEL), mod_map, pipeline_mode=pl.Buffered(1)),
            _resident((1, D_MODEL), const2),
            _resident((1, D_MODEL), const2),
            _resident(wct.shape, const2),
            pl.BlockSpec((PEER_EC, D_MODEL), lambda i, e: (e, 0)),
            pl.BlockSpec((D_MODEL, PEER_EC), lambda i, e: (0, e)),
        ],
        out_specs=pl.BlockSpec((tb, D_MODEL), lambda i, e: (i, 0)),
        out_shape=jax.ShapeDtypeStruct((n_tok, D_MODEL), F32),
        scratch_shapes=[
            pltpu.VMEM((D_MODEL, tb), BF16),
            pltpu.VMEM((tb // LANES, 2 * PEER_HEADS * PEER_NKEYS, LANES), F32),
            pltpu.VMEM((tb // LANES, PEER_HEADS, PEER_NKEYS, LANES), F32),
            pltpu.VMEM((tb // LANES, PEER_HEADS, PEER_NKEYS, LANES), F32),
            pltpu.VMEM((tb // LANES, PEER_HEADS, PEER_NKEYS, LANES), F32),
            pltpu.VMEM((tb // LANES, PEER_EC, LANES), F32),
            pltpu.VMEM((PEER_EC, tb), BF16),
            pltpu.VMEM((D_MODEL, tb), F32),
        ],
        compiler_params=pltpu.CompilerParams(
            dimension_semantics=("arbitrary", "arbitrary"), vmem_limit_bytes=VMEM_LIMIT_BYTES),
        name="peer_prompt" if nseq == 1 else "peer_sample",
    )(x2d, mod3, npre, npost, wct, u_bf, vt_bf)


def kernel(x_prompt, x_sample, c_prompt, c_sample, state_conv, state_gla, norm_mix_pre, norm_mix_post, norm_ffn_pre, norm_ffn_post, w_ada, b_ada, w_in, conv_w, conv_b, w_alpha, b_alpha, gla_norm_w, w_proj_a, w_proj_b, w_out, peer_wq, peer_subkeys, peer_u, peer_v):
    depth = w_in.shape[0]
    assert depth == 1, "single-layer trunk"
    n_p, t_p, _ = x_prompt.shape
    n_s, t_s, _ = x_sample.shape
    l = 0

    c_all = jnp.concatenate([c_prompt, c_sample], axis=0)
    mod = _ada_call(c_all, w_ada[l], b_ada[l])
    mod_p = mod[:n_p].reshape(n_p, 1, N_MOD * D_MODEL)
    mod_s = mod[n_p:].reshape(n_s, 1, N_MOD * D_MODEL)

    w_in_main = w_in[l][:, :W_IN_MAIN].astype(BF16)
    w_al = jnp.pad(w_in[l][:, W_IN_MAIN:], ((0, 0), (0, LANES - GLA_GATE_RANK))).astype(BF16)
    w_alpha_p = jnp.pad(w_alpha[l], ((0, LANES - GLA_GATE_RANK), (0, 0))).astype(BF16)
    row = lambda a: a.reshape(1, -1)
    mixer_w = (row(norm_mix_pre[l]), row(norm_mix_post[l]), w_in_main, w_al, conv_w[l], row(conv_b[l]),
               w_alpha_p, row(b_alpha[l]), row(gla_norm_w[l]), w_proj_a[l].astype(BF16),
               w_proj_b[l].astype(BF16), w_out[l].astype(BF16))

    zeros_conv = jnp.zeros((n_p, CONV_W - 1, CONV_DIM), state_conv.dtype)
    zeros_gla = jnp.zeros((n_p, GLA_HEADS, GLA_HEAD_K, GLA_HEAD_V), state_gla.dtype)
    x1_p, conv_p, gla_p = _mixer_call(
        x_prompt.reshape(n_p * t_p, D_MODEL), mod_p, zeros_conv, zeros_gla, mixer_w,
        nseq_total=n_p, seq_len=t_p, carried=True)
    x1_s, conv_s, gla_s = _mixer_call(
        x_sample.reshape(n_s * t_s, D_MODEL), mod_s, state_conv[l], state_gla[l], mixer_w,
        nseq_total=n_s, seq_len=t_s, carried=False)

    wct = _wc_call(peer_subkeys[l], peer_wq[l])
    u_bf = peer_u[l].astype(BF16)
    vt_bf = peer_v[l].astype(BF16).T
    npre2, npost2 = row(norm_ffn_pre[l]), row(norm_ffn_post[l])
    y_p = _peer_call(x1_p, mod_p, npre2, npost2, wct, u_bf, vt_bf, nseq_total=n_p, seq_len=t_p)
    y_s = _peer_call(x1_s, mod_s, npre2, npost2, wct, u_bf, vt_bf, nseq_total=n_s, seq_len=t_s)

    return (y_p.reshape(n_p, t_p, D_MODEL), y_s.reshape(n_s, t_s, D_MODEL),
            conv_p[None], gla_p[None], conv_s[None], gla_s[None])
```

```python
import functools
import math

import jax
import jax.numpy as jnp
import numpy as np
from jax import lax
from jax.experimental import pallas as pl
from jax.experimental.pallas import tpu as pltpu

F32 = jnp.float32
BF16 = jnp.bfloat16

D_MODEL = 1024
CONV_DIM = 512
CONV_W = 3
GLA_HEADS = 4
GLA_DK = 512
GLA_DV = 1024
GLA_HEAD_K = 128
GLA_HEAD_V = 256
GLA_GATE_RANK = 16
GLA_GATE_TAU = 16.0
PEER_HEADS = 8
PEER_NKEYS = 128
PEER_HALF = 128
PEER_TOPK = 16
N_MOD = 6
EPS = 1e-6
W_IN_MAIN = 3 * CONV_DIM + 2 * GLA_DK + 2 * GLA_DV + 2 * D_MODEL
OFF_BG, OFF_CG, OFF_HIN = 0, 512, 1024
OFF_Q, OFF_K, OFF_V, OFF_R, OFF_GA, OFF_GB = 1536, 2048, 2560, 3584, 4608, 5632

LANES = 128
SUBLANES = 8
VMEM_LIMIT_BYTES = 58 * 1024 * 1024

GLA_CHUNK = 128
MIX_TB_PROMPT = 512
MIX_SEQ_SAMPLE = 16
PEER_TB = 512
PEER_EC = 2048
PEER_I1 = PEER_EC // PEER_NKEYS
GELU_C0 = math.sqrt(2.0 / math.pi)
GELU_C1 = 0.044715


def _dot(a, b):
    return jnp.dot(a, b, preferred_element_type=F32)


def _dot_nt(a, b):
    return lax.dot_general(a, b, (((1,), (1,)), ((), ())), preferred_element_type=F32)


def _dot_tn(a, b):
    return lax.dot_general(a, b, (((0,), (0,)), ((), ())), preferred_element_type=F32)


def _split2(x):
    p0 = x.astype(BF16)
    return p0, (x - p0.astype(F32)).astype(BF16)


def _rms(x, w):
    return x * lax.rsqrt(jnp.mean(x * x, axis=-1, keepdims=True) + EPS) * w


def _sigmoid(x):
    return 1.0 / (1.0 + jnp.exp(-x))


def _resident(shape, index_map):
    return pl.BlockSpec(shape, index_map, pipeline_mode=pl.Buffered(1))


def _rows_from_segments(seg_vals, seg_len):
    nseg, _, d = seg_vals.shape
    return jnp.broadcast_to(seg_vals, (nseg, seg_len, d)).reshape(nseg * seg_len, d)


def _ada_kernel(c_ref, w_ref, b_ref, o_ref):
    c = c_ref[...]
    a = (c * _sigmoid(c)).astype(BF16)
    o_ref[...] = _dot(a, w_ref[...].astype(BF16)) + b_ref[...]


def _ada_call(c_all, w_ada, b_ada):
    rows = c_all.shape[0]
    ncols = w_ada.shape[1]
    bn = 1536
    return pl.pallas_call(
        _ada_kernel,
        grid=(ncols // bn,),
        in_specs=[
            pl.BlockSpec((rows, D_MODEL), lambda j: (0, 0)),
            pl.BlockSpec((D_MODEL, bn), lambda j: (0, j)),
            pl.BlockSpec((1, bn), lambda j: (0, j)),
        ],
        out_specs=pl.BlockSpec((rows, bn), lambda j: (0, j)),
        out_shape=jax.ShapeDtypeStruct((rows, ncols), F32),
        compiler_params=pltpu.CompilerParams(
            dimension_semantics=("arbitrary",), vmem_limit_bytes=VMEM_LIMIT_BYTES),
        name="ada_mod",
    )(c_all, w_ada, b_ada.reshape(1, ncols))


def _gla_levels(seg_len):
    return [b for b in (1, 2, 4, 8, 16, 32, 64) if 2 * b <= seg_len]


def _gla_consts(chunk, seg_len):
    t = np.arange(chunk)
    seg = t // seg_len
    blocks = [(t[None, :] <= t[:, None]) & (seg[None, :] == seg[:, None])]
    masks = []
    for b in _gla_levels(seg_len):
        parent = t // (2 * b)
        refpos = parent * (2 * b) + b - 1
        upper = (t % (2 * b)) >= b
        a = np.zeros((chunk, chunk), bool)
        for i in range(chunk):
            if upper[i]:
                a[i, refpos[i] + 1:i + 1] = True
            else:
                a[i, i + 1:refpos[i] + 1] = True
        blocks.append(a)
        masks.append((parent[:, None] == parent[None, :]) & upper[:, None] & ~upper[None, :])
    segend = (seg + 1) * seg_len - 1
    a = np.zeros((chunk, chunk), bool)
    for i in range(chunk):
        a[i, i + 1:segend[i] + 1] = True
    blocks.append(a)
    masks.append(np.eye(chunk, dtype=bool))
    segones = np.zeros((chunk, LANES), np.float32)
    segones[t, seg] = 1.0
    return (jnp.asarray(np.concatenate(blocks, 0).astype(np.float32), BF16),
            jnp.asarray(np.stack(masks).astype(np.float32)),
            jnp.asarray(segones, BF16))


def _gla_chunk(q, k, v, la, a_all, masks, segones, states, seg_len):
    c = q.shape[0]
    nseg = c // seg_len
    nl = len(_gla_levels(seg_len))
    pieces = _split2(la)
    d_all = sum(_dot(a_all, p) for p in pieces)
    e_all = jnp.exp(d_all)
    cum_last_t = sum(_dot_tn(p, segones) for p in pieces)
    decay_cols = jnp.exp(cum_last_t)
    e_cum = e_all[0:c]
    e_last = e_all[(nl + 1) * c:(nl + 2) * c]
    outs = []
    new_states = [[None] * GLA_HEADS for _ in range(nseg)]
    for h in range(GLA_HEADS):
        ks = slice(h * GLA_HEAD_K, (h + 1) * GLA_HEAD_K)
        vs = slice(h * GLA_HEAD_V, (h + 1) * GLA_HEAD_V)
        qh, kh = q[:, ks], k[:, ks]
        vh = v[:, vs].astype(BF16)
        scores = _dot_nt(qh.astype(BF16), kh.astype(BF16)) * masks[nl]
        for li in range(nl):
            eb = e_all[(1 + li) * c:(2 + li) * c, ks]
            scores = scores + _dot_nt((qh * eb).astype(BF16), (kh * eb).astype(BF16)) * masks[li]
        o_h = _dot(scores.astype(BF16), vh)
        qt = qh * e_cum[:, ks]
        kt = kh * e_last[:, ks]
        vf = v[:, vs]
        inter = []
        for s in range(nseg):
            rows = slice(s * seg_len, (s + 1) * seg_len)
            st = states[s][h]
            inter.append(_dot(qt[rows].astype(BF16), st.astype(BF16)))
            new_states[s][h] = decay_cols[ks, s:s + 1] * st + _dot_tn(
                kt[rows].astype(BF16), vf[rows].astype(BF16))
        o_h = o_h + (inter[0] if nseg == 1 else jnp.concatenate(inter, axis=0))
        outs.append(o_h)
    return jnp.concatenate(outs, axis=1), new_states


def _mixer_kernel(cfg, x_ref, mod_ref, cst_ref, gst_ref, npre_ref, npost_ref, win_ref, wal_ref,
                  convw_ref, convb_ref, walpha_ref, balpha_ref, gnorm_ref, wpa_ref, wpb_ref,
                  wout_ref, aall_ref, masks_ref, segones_ref,
                  xo_ref, cnew_ref, gnew_ref, proj_ref, o_ref, carry_ref, state_ref):
    tb, nseq, carried = cfg["tb"], cfg["nseq"], cfg["carried"]
    lseq = tb // nseq
    chunk = min(GLA_CHUNK, tb)
    seg_len = min(lseq, chunk)
    step = pl.program_id(1)

    if carried:
        @pl.when(step == 0)
        def _():
            carry_ref[...] = jnp.zeros_like(carry_ref)
            state_ref[...] = jnp.zeros_like(state_ref)

    x = x_ref[...]
    mod = mod_ref[...]
    sh = _rows_from_segments(mod[:, :, 0:D_MODEL], lseq)
    sc = _rows_from_segments(mod[:, :, D_MODEL:2 * D_MODEL], lseq)
    gate = _rows_from_segments(mod[:, :, 2 * D_MODEL:3 * D_MODEL], lseq)
    h = (_rms(x, npre_ref[...]) * (1.0 + sc) + sh).astype(BF16)
    proj_ref[...] = _dot(h, win_ref[...])
    alow = _dot(h, wal_ref[...])

    u = proj_ref[:, OFF_CG:OFF_CG + CONV_DIM] * proj_ref[:, OFF_HIN:OFF_HIN + CONV_DIM]
    prev = carry_ref[...] if carried else cst_ref[...]
    prev0 = _rows_from_segments(prev[:, 0:1, :], lseq)
    prev1 = _rows_from_segments(prev[:, 1:2, :], lseq)
    pos = lax.broadcasted_iota(jnp.int32, (tb, 1), 0) % lseq
    u1 = jnp.where(pos == 0, prev1, pltpu.roll(u, 1, 0))
    u2 = jnp.where(pos == 0, prev0, jnp.where(pos == 1, prev1, pltpu.roll(u, 2, 0)))
    cw = convw_ref[...]
    y_conv = convb_ref[...] + cw[2:3, :] * u + cw[0:1, :] * u2 + cw[1:2, :] * u1
    new_buf = u.reshape(nseq, lseq, CONV_DIM)[:, lseq - 2:lseq, :]
    if carried:
        carry_ref[...] = new_buf
    cnew_ref[...] = new_buf
    branch_a = _dot((proj_ref[:, OFF_BG:OFF_BG + CONV_DIM] * y_conv).astype(BF16), wpa_ref[...])

    z = _dot(alow.astype(BF16), walpha_ref[...]) + balpha_ref[...]
    log_a = (jnp.minimum(z, 0.0) - jnp.log1p(jnp.exp(-jnp.abs(z)))) * (1.0 / GLA_GATE_TAU)
    a_all = aall_ref[...]
    masks = [masks_ref[i] for i in range(masks_ref.shape[0])]
    segones = segones_ref[...]
    nseg = chunk // seg_len
    for c in range(tb // chunk):
        rows = slice(c * chunk, (c + 1) * chunk)
        if carried:
            states = [[state_ref[hh] for hh in range(GLA_HEADS)]]
        else:
            states = [[gst_ref[c * nseg + s, hh] for hh in range(GLA_HEADS)] for s in range(nseg)]
        o_c, new_states = _gla_chunk(
            proj_ref[rows, OFF_Q:OFF_Q + GLA_DK] * (GLA_HEAD_K ** -0.5),
            proj_ref[rows, OFF_K:OFF_K + GLA_DK],
            proj_ref[rows, OFF_V:OFF_V + GLA_DV],
            log_a[rows], a_all, masks, segones, states, seg_len)
        o_ref[rows, :] = o_c
        for s in range(nseg):
            for hh in range(GLA_HEADS):
                if carried:
                    state_ref[hh] = new_states[s][hh]
                else:
                    gnew_ref[c * nseg + s, hh] = new_states[s][hh]
    if carried:
        gnew_ref[0] = state_ref[...]

    gn = gnorm_ref[...]
    normed = []
    for hh in range(GLA_HEADS):
        vs = slice(hh * GLA_HEAD_V, (hh + 1) * GLA_HEAD_V)
        normed.append(_rms(o_ref[:, vs], gn[:, vs]))
    r = proj_ref[:, OFF_R:OFF_R + GLA_DV]
    ob = jnp.concatenate(normed, axis=1) * (r * _sigmoid(r))
    branch_b = _dot(ob.astype(BF16), wpb_ref[...])

    merged = (_sigmoid(proj_ref[:, OFF_GA:OFF_GA + D_MODEL]) * branch_a
              + _sigmoid(proj_ref[:, OFF_GB:OFF_GB + D_MODEL]) * branch_b)
    m = _dot(merged.astype(BF16), wout_ref[...])
    xo_ref[...] = x + gate * _rms(m, npost_ref[...])


def _mixer_call(x2d, mod3, conv_state, gla_state, weights, *, nseq_total, seq_len, carried):
    if carried:
        tb, nseq = MIX_TB_PROMPT, 1
        grid = (nseq_total, seq_len // tb)
        x_map = lambda b, c: (b * (seq_len // tb) + c, 0)
    else:
        nseq = MIX_SEQ_SAMPLE
        tb = nseq * seq_len
        grid = (nseq_total // nseq, 1)
        x_map = lambda b, c: (b, 0)
    chunk = min(GLA_CHUNK, tb)
    seg_len = min(tb // nseq, chunk)
    a_all, masks, segones = _gla_consts(chunk, seg_len)
    cfg = dict(tb=tb, nseq=nseq, carried=carried)
    const2 = lambda b, c: (0, 0)
    const3 = lambda b, c: (0, 0, 0)
    (npre, npost, w_in_main, w_al, conv_w, conv_b, w_alpha, b_alpha, gnorm, wpa, wpb, wout) = weights
    n_tok = x2d.shape[0]
    state_spec = pl.BlockSpec(
        (nseq, GLA_HEADS, GLA_HEAD_K, GLA_HEAD_V), lambda b, c: (b, 0, 0, 0),
        **({} if carried else dict(pipeline_mode=pl.Buffered(1))))
    in_specs = [
        pl.BlockSpec((tb, D_MODEL), x_map),
        pl.BlockSpec((nseq, 1, 3 * D_MODEL), lambda b, c: (b, 0, 0)),
        pl.BlockSpec((nseq, CONV_W - 1, CONV_DIM), lambda b, c: (b, 0, 0)),
        state_spec,
        _resident((1, D_MODEL), const2),
        _resident((1, D_MODEL), const2),
        _resident((D_MODEL, W_IN_MAIN), const2),
        _resident((D_MODEL, LANES), const2),
        _resident((CONV_W, CONV_DIM), const2),
        _resident((1, CONV_DIM), const2),
        _resident((LANES, GLA_DK), const2),
        _resident((1, GLA_DK), const2),
        _resident((1, GLA_DV), const2),
        _resident((CONV_DIM, D_MODEL), const2),
        _resident((GLA_DV, D_MODEL), const2),
        _resident((D_MODEL, D_MODEL), const2),
        _resident(a_all.shape, const2),
        _resident(masks.shape, const3),
        _resident(segones.shape, const2),
    ]
    out_specs = [
        pl.BlockSpec((tb, D_MODEL), x_map),
        pl.BlockSpec((nseq, CONV_W - 1, CONV_DIM), lambda b, c: (b, 0, 0)),
        state_spec,
    ]
    out_shape = [
        jax.ShapeDtypeStruct((n_tok, D_MODEL), F32),
        jax.ShapeDtypeStruct((nseq_total, CONV_W - 1, CONV_DIM), F32),
        jax.ShapeDtypeStruct((nseq_total, GLA_HEADS, GLA_HEAD_K, GLA_HEAD_V), F32),
    ]
    scratch = [
        pltpu.VMEM((tb, W_IN_MAIN), F32),
        pltpu.VMEM((tb, GLA_DV), F32),
        pltpu.VMEM((1, CONV_W - 1, CONV_DIM), F32),
        pltpu.VMEM((GLA_HEADS, GLA_HEAD_K, GLA_HEAD_V), F32),
    ]
    return pl.pallas_call(
        functools.partial(_mixer_kernel, cfg),
        grid=grid,
        in_specs=in_specs,
        out_specs=out_specs,
        out_shape=out_shape,
        scratch_shapes=scratch,
        compiler_params=pltpu.CompilerParams(
            dimension_semantics=("arbitrary", "arbitrary"), vmem_limit_bytes=VMEM_LIMIT_BYTES),
        name="mixer_prompt" if carried else "mixer_sample",
    )(x2d, mod3, conv_state, gla_state, npre, npost, w_in_main, w_al, conv_w, conv_b, w_alpha,
      b_alpha, gnorm, wpa, wpb, wout, a_all, masks, segones)


def _wc_kernel(sk_ref, wq_ref, o_ref):
    o_ref[0] = lax.dot_general(sk_ref[0], wq_ref[...], (((1,), (1,)), ((), ())),
                               precision=lax.Precision.HIGHEST,
                               preferred_element_type=F32).astype(BF16)


def _wc_call(subkeys, wq):
    nhp = PEER_HEADS * 2
    out = pl.pallas_call(
        _wc_kernel,
        grid=(nhp,),
        in_specs=[
            pl.BlockSpec((1, PEER_NKEYS, PEER_HALF), lambda i: (i, 0, 0)),
            pl.BlockSpec((D_MODEL, PEER_HALF), lambda i: (0, i)),
        ],
        out_specs=pl.BlockSpec((1, PEER_NKEYS, D_MODEL), lambda i: (i, 0, 0)),
        out_shape=jax.ShapeDtypeStruct((nhp, PEER_NKEYS, D_MODEL), BF16),
        compiler_params=pltpu.CompilerParams(
            dimension_semantics=("arbitrary",), vmem_limit_bytes=VMEM_LIMIT_BYTES),
        name="peer_wc",
    )(subkeys.reshape(nhp, PEER_NKEYS, PEER_HALF), wq)
    return out.reshape(PEER_HEADS, 2, PEER_NKEYS, D_MODEL).transpose(1, 0, 2, 3).reshape(
        2 * PEER_HEADS * PEER_NKEYS, D_MODEL)


def _cand_pairs():
    n = PEER_TOPK + 1
    return [(a, b) for a in range(n) for b in range(n) if (a + 1) * (b + 1) <= n]


def _tree_reduce(op, xs):
    xs = list(xs)
    while len(xs) > 1:
        xs = [op(xs[i], xs[i + 1]) if i + 1 < len(xs) else xs[i] for i in range(0, len(xs), 2)]
    return xs[0]


def _desc_distinct(vals, count):
    out = [_tree_reduce(jnp.maximum, vals)]
    for _ in range(count - 1):
        m = out[-1]
        out.append(_tree_reduce(jnp.maximum, [jnp.where(v < m, v, -jnp.inf) for v in vals]))
    return out


def _store_gate_factors(st_ref, e1_ref, e2_ref, lg, h, max1, max2, half_inv_z_row):
    s1 = st_ref[lg, h * PEER_NKEYS:(h + 1) * PEER_NKEYS, :]
    row2 = (PEER_HEADS + h) * PEER_NKEYS
    s2 = st_ref[lg, row2:row2 + PEER_NKEYS, :]
    e1_ref[lg, h] = jnp.exp(s1 - max1) * half_inv_z_row
    e2_ref[lg, h] = jnp.exp(s2 - max2)


def _peer_select(st_ref, c1_ref, e1_ref, e2_ref, lg):
    nk = PEER_TOPK + 1
    tops = [[None] * PEER_HEADS for _ in range(2)]
    bad = jnp.zeros((1, LANES), F32)
    for p in range(2):
        for h in range(PEER_HEADS):
            r0 = (p * PEER_HEADS + h) * PEER_NKEYS
            s = st_ref[lg, r0:r0 + PEER_NKEYS, :]
            out = [jnp.max(s, axis=0, keepdims=True)]
            for _ in range(nk - 1):
                out.append(jnp.max(jnp.where(s < out[-1], s, -jnp.inf), axis=0, keepdims=True))
            tops[p][h] = out
            n_top = jnp.sum(jnp.where(s >= out[-1], 1.0, 0.0), axis=0, keepdims=True)
            bad = jnp.maximum(bad, jnp.where(n_top != float(nk), 1.0, 0.0))
    packed = [[jnp.concatenate([tops[p][h][k] for h in range(PEER_HEADS)], axis=0)
               for k in range(nk)] for p in range(2)]
    cands = [packed[0][a] + packed[1][b] for (a, b) in _cand_pairs()]
    t = _desc_distinct(cands, nk)
    n_top = _tree_reduce(jnp.add, [jnp.where(c >= t[-1], 1.0, 0.0) for c in cands])
    bad = jnp.maximum(bad, jnp.max(jnp.where(n_top != float(nk), 1.0, 0.0), axis=0, keepdims=True))
    tau = 0.5 * (t[PEER_TOPK - 1] + t[PEER_TOPK])
    z = jnp.ones_like(t[0])
    for k in range(1, PEER_TOPK):
        z = z + jnp.exp(t[k] - t[0])
    half_inv_z = 0.5 / z
    for h in range(PEER_HEADS):
        s1 = st_ref[lg, h * PEER_NKEYS:(h + 1) * PEER_NKEYS, :]
        c1_ref[lg, h] = tau[h:h + 1, :] - s1
        _store_gate_factors(st_ref, e1_ref, e2_ref, lg, h, tops[0][h][0], tops[1][h][0],
                            half_inv_z[h:h + 1, :])

    @pl.when(jnp.max(bad) > 0.0)
    def _():
        for h in range(PEER_HEADS):
            row2 = (PEER_HEADS + h) * PEER_NKEYS
            s2 = st_ref[lg, row2:row2 + PEER_NKEYS, :]
            c1 = c1_ref[lg, h]
            n_sel = jnp.zeros((PEER_NKEYS, LANES), F32)
            for k in range(nk):
                v = tops[1][h][k]
                mult = jnp.sum(jnp.where(s2 == v, 1.0, 0.0), axis=0, keepdims=True)
                n_sel = n_sel + jnp.where(v >= c1, mult, 0.0)
            total = jnp.sum(n_sel, axis=0, keepdims=True)

            @pl.when(jnp.max(jnp.where(total != float(PEER_TOPK), 1.0, 0.0)) > 0.0)
            def _(h=h):
                _peer_select_exact(st_ref, c1_ref, e1_ref, e2_ref, lg, h)


def _peer_select_exact(st_ref, c1_ref, e1_ref, e2_ref, lg, h):
    n_pairs = PEER_TOPK * PEER_TOPK
    row1 = h * PEER_NKEYS
    row2 = (PEER_HEADS + h) * PEER_NKEYS
    s1 = st_ref[lg, row1:row1 + PEER_NKEYS, :]
    s2 = st_ref[lg, row2:row2 + PEER_NKEYS, :]
    key_id = lax.broadcasted_iota(jnp.int32, (1, PEER_NKEYS, LANES), 1).astype(F32)
    slot_id = lax.broadcasted_iota(jnp.int32, (1, PEER_TOPK, 1), 1)

    def pick(k, state):
        rem, rank, vals = state
        m = jnp.max(rem, axis=1, keepdims=True)
        first = jnp.min(jnp.where(rem == m, key_id, float(PEER_NKEYS)), axis=1, keepdims=True)
        hit = key_id == first
        return (jnp.where(hit, -jnp.inf, rem),
                jnp.where(hit, lax.convert_element_type(k, F32), rank),
                jnp.where(slot_id == k, m, vals))
    _, rank, vals = lax.fori_loop(
        0, PEER_TOPK, pick,
        (jnp.stack([s1, s2]), jnp.full((2, PEER_NKEYS, LANES), float(PEER_TOPK), F32),
         jnp.zeros((2, PEER_TOPK, LANES), F32)))

    cand = (vals[0][:, None, :] + vals[1][None, :, :]).reshape(n_pairs, LANES)
    flat = lax.broadcasted_iota(jnp.int32, (n_pairs, 1), 0).astype(F32)

    def pick_pair(k, state):
        rem, taken, z, t0 = state
        m = jnp.max(rem, axis=0, keepdims=True)
        first = jnp.min(jnp.where(rem == m, flat, float(n_pairs)), axis=0, keepdims=True)
        hit = flat == first
        taken = taken + jnp.sum(jnp.where(hit, 1.0, 0.0).reshape(PEER_TOPK, PEER_TOPK, LANES), axis=1)
        t0 = jnp.where(k == 0, m, t0)
        return jnp.where(hit, -jnp.inf, rem), taken, z + jnp.exp(m - t0), t0
    zeros = jnp.zeros((1, LANES), F32)
    _, taken, z, _ = lax.fori_loop(0, PEER_TOPK, pick_pair,
                                   (cand, jnp.zeros((PEER_TOPK, LANES), F32), zeros, zeros))
    n1 = jnp.zeros((PEER_NKEYS, LANES), F32)
    for a in range(PEER_TOPK):
        n1 = n1 + jnp.where(rank[0] == float(a), taken[a:a + 1, :], 0.0)
    e1_ref[lg, h] = jnp.exp(s1 - vals[0][0:1, :]) * (0.5 / z)
    e2_ref[lg, h] = jnp.exp(s2 - vals[1][0:1, :])
    c1_ref[lg, h] = 0.5 - n1
    st_ref[lg, row2:row2 + PEER_NKEYS, :] = -rank[1]


def _peer_kernel(cfg, x_ref, mod_ref, npre_ref, npost_ref, wct_ref, u_ref, vt_ref, y_ref,
                 h2_ref, st_ref, c1_ref, e1_ref, e2_ref, a_ref, w_ref, acc_ref):
    tb, nseq = cfg["tb"], cfg["nseq"]
    lseq = tb // nseq
    e = pl.program_id(1)
    n_e = pl.num_programs(1)
    n_lg = tb // LANES

    def lane_group_rows(seg_vals, lg):
        if nseq == 1:
            return _rows_from_segments(seg_vals, LANES)
        per_lg = nseq // n_lg
        return _rows_from_segments(seg_vals[lg * per_lg:(lg + 1) * per_lg], lseq)

    @pl.when(e == 0)
    def _():
        mod = mod_ref[...]
        for lg in range(n_lg):
            sh = lane_group_rows(mod[:, :, 0:D_MODEL], lg)
            sc = lane_group_rows(mod[:, :, D_MODEL:2 * D_MODEL], lg)
            h2 = _rms(x_ref[lg * LANES:(lg + 1) * LANES, :], npre_ref[...]) * (1.0 + sc) + sh
            h2_ref[:, lg * LANES:(lg + 1) * LANES] = h2.T.astype(BF16)
        score_rows = 2 * PEER_NKEYS
        for r in range(0, 2 * PEER_HEADS * PEER_NKEYS, score_rows):
            st = _dot(wct_ref[r:r + score_rows, :], h2_ref[...])
            for lg in range(n_lg):
                st_ref[lg, r:r + score_rows, :] = st[:, lg * LANES:(lg + 1) * LANES]

        def select(lg, carry):
            _peer_select(st_ref, c1_ref, e1_ref, e2_ref, lg)
            return carry
        lax.fori_loop(0, n_lg, select, 0)
        acc_ref[...] = jnp.zeros_like(acc_ref)

    a_all = _dot(u_ref[...], h2_ref[...])
    for lg in range(n_lg):
        a_ref[lg] = a_all[:, lg * LANES:(lg + 1) * LANES]
    i1_tile = pl.ds(pl.multiple_of(e * PEER_I1, SUBLANES), PEER_I1)

    def gate_lane_group(lg, carry):
        lanes = pl.ds(pl.multiple_of(lg * LANES, LANES), LANES)
        c1t = [c1_ref[lg, h, i1_tile, :] for h in range(PEER_HEADS)]
        e1t = [e1_ref[lg, h, i1_tile, :] for h in range(PEER_HEADS)]
        for i1 in range(PEER_I1):
            g = jnp.zeros((PEER_NKEYS, LANES), F32)
            for h in range(PEER_HEADS):
                row2 = (PEER_HEADS + h) * PEER_NKEYS
                s2 = st_ref[lg, row2:row2 + PEER_NKEYS, :]
                p = e2_ref[lg, h] * e1t[h][i1:i1 + 1, :]
                g = g + jnp.where(s2 >= c1t[h][i1:i1 + 1, :], p, 0.0)
            rows = slice(i1 * PEER_NKEYS, (i1 + 1) * PEER_NKEYS)
            a = a_ref[lg, rows, :]
            act2 = a + a * jnp.tanh(a * (GELU_C0 + (GELU_C0 * GELU_C1) * (a * a)))
            w_ref[rows, lanes] = (act2 * g).astype(BF16)
        return carry
    lax.fori_loop(0, n_lg, gate_lane_group, 0)
    acc_ref[...] += _dot(vt_ref[...], w_ref[...])

    @pl.when(e == n_e - 1)
    def _():
        mod = mod_ref[...]
        for lg in range(n_lg):
            rows = slice(lg * LANES, (lg + 1) * LANES)
            gate = lane_group_rows(mod[:, :, 2 * D_MODEL:3 * D_MODEL], lg)
            out = acc_ref[:, rows].T
            y_ref[rows, :] = x_ref[rows, :] + gate * _rms(out, npost_ref[...])


def _peer_call(x2d, mod3, npre, npost, wct, u_bf, vt_bf, *, nseq_total, seq_len):
    n_tok = x2d.shape[0]
    tb = PEER_TB
    if seq_len >= tb:
        nseq = 1
        mod_map = lambda i, e: (i // (seq_len // tb), 0, 1)
    else:
        nseq = tb // seq_len
        mod_map = lambda i, e: (i, 0, 1)
    n_exp = u_bf.shape[0]
    assert n_exp % PEER_EC == 0 and PEER_I1 % SUBLANES == 0
    cfg = dict(tb=tb, nseq=nseq)
    const2 = lambda i, e: (0, 0)
    return pl.pallas_call(
        functools.partial(_peer_kernel, cfg),
        grid=(n_tok // tb, n_exp // PEER_EC),
        in_specs=[
            pl.BlockSpec((tb, D_MODEL), lambda i, e: (i, 0), pipeline_mode=pl.Buffered(1)),
            pl.BlockSpec((nseq, 1, 3 * D_MODEL), mod_map, pipeline_mode=pl.Buffered(1)),
            _resident((1, D_MODEL), const2),
            _resident((1, D_MODEL), const2),
            _resident(wct.shape, const2),
            pl.BlockSpec((PEER_EC, D_MODEL), lambda i, e: (e, 0)),
            pl.BlockSpec((D_MODEL, PEER_EC), lambda i, e: (0, e)),
        ],
        out_specs=pl.BlockSpec((tb, D_MODEL), lambda i, e: (i, 0)),
        out_shape=jax.ShapeDtypeStruct((n_tok, D_MODEL), F32),
        scratch_shapes=[
            pltpu.VMEM((D_MODEL, tb), BF16),
            pltpu.VMEM((tb // LANES, 2 * PEER_HEADS * PEER_NKEYS, LANES), F32),
            pltpu.VMEM((tb // LANES, PEER_HEADS, PEER_NKEYS, LANES), F32),
            pltpu.VMEM((tb // LANES, PEER_HEADS, PEER_NKEYS, LANES), F32),
            pltpu.VMEM((tb // LANES, PEER_HEADS, PEER_NKEYS, LANES), F32),
            pltpu.VMEM((tb // LANES, PEER_EC, LANES), F32),
            pltpu.VMEM((PEER_EC, tb), BF16),
            pltpu.VMEM((D_MODEL, tb), F32),
        ],
        compiler_params=pltpu.CompilerParams(
            dimension_semantics=("arbitrary", "arbitrary"), vmem_limit_bytes=VMEM_LIMIT_BYTES),
        name="peer_prompt" if nseq == 1 else "peer_sample",
    )(x2d, mod3, npre, npost, wct, u_bf, vt_bf)


def kernel(x_prompt, x_sample, c_prompt, c_sample, state_conv, state_gla, norm_mix_pre, norm_mix_post, norm_ffn_pre, norm_ffn_post, w_ada, b_ada, w_in, conv_w, conv_b, w_alpha, b_alpha, gla_norm_w, w_proj_a, w_proj_b, w_out, peer_wq, peer_subkeys, peer_u, peer_v):
    depth = w_in.shape[0]
    assert depth == 1, "single-layer trunk"
    n_p, t_p, _ = x_prompt.shape
    n_s, t_s, _ = x_sample.shape
    l = 0

    c_all = jnp.concatenate([c_prompt, c_sample], axis=0)
    mod = _ada_call(c_all, w_ada[l], b_ada[l])
    mod_p = mod[:n_p].reshape(n_p, 1, N_MOD * D_MODEL)
    mod_s = mod[n_p:].reshape(n_s, 1, N_MOD * D_MODEL)

    w_in_main = w_in[l][:, :W_IN_MAIN].astype(BF16)
    w_al = jnp.pad(w_in[l][:, W_IN_MAIN:], ((0, 0), (0, LANES - GLA_GATE_RANK))).astype(BF16)
    w_alpha_p = jnp.pad(w_alpha[l], ((0, LANES - GLA_GATE_RANK), (0, 0))).astype(BF16)
    row = lambda a: a.reshape(1, -1)
    mixer_w = (row(norm_mix_pre[l]), row(norm_mix_post[l]), w_in_main, w_al, conv_w[l], row(conv_b[l]),
               w_alpha_p, row(b_alpha[l]), row(gla_norm_w[l]), w_proj_a[l].astype(BF16),
               w_proj_b[l].astype(BF16), w_out[l].astype(BF16))

    zeros_conv = jnp.zeros((n_p, CONV_W - 1, CONV_DIM), state_conv.dtype)
    zeros_gla = jnp.zeros((n_p, GLA_HEADS, GLA_HEAD_K, GLA_HEAD_V), state_gla.dtype)
    x1_p, conv_p, gla_p = _mixer_call(
        x_prompt.reshape(n_p * t_p, D_MODEL), mod_p, zeros_conv, zeros_gla, mixer_w,
        nseq_total=n_p, seq_len=t_p, carried=True)
    x1_s, conv_s, gla_s = _mixer_call(
        x_sample.reshape(n_s * t_s, D_MODEL), mod_s, state_conv[l], state_gla[l], mixer_w,
        nseq_total=n_s, seq_len=t_s, carried=False)

    wct = _wc_call(peer_subkeys[l], peer_wq[l])
    u_bf = peer_u[l].astype(BF16)
    vt_bf = peer_v[l].astype(BF16).T
    npre2, npost2 = row(norm_ffn_pre[l]), row(norm_ffn_post[l])
    y_p = _peer_call(x1_p, mod_p, npre2, npost2, wct, u_bf, vt_bf, nseq_total=n_p, seq_len=t_p)
    y_s = _peer_call(x1_s, mod_s, npre2, npost2, wct, u_bf, vt_bf, nseq_total=n_s, seq_len=t_s)

    return (y_p.reshape(n_p, t_p, D_MODEL), y_s.reshape(n_s, t_s, D_MODEL),
            conv_p[None], gla_p[None], conv_s[None], gla_s[None])
```

```python
import functools
import math

import jax
import jax.numpy as jnp
import numpy as np
from jax import lax
from jax.experimental import pallas as pl
from jax.experimental.pallas import tpu as pltpu

F32 = jnp.float32
BF16 = jnp.bfloat16

D_MODEL = 1024
CONV_DIM = 512
CONV_W = 3
GLA_HEADS = 4
GLA_DK = 512
GLA_DV = 1024
GLA_HEAD_K = 128
GLA_HEAD_V = 256
GLA_GATE_RANK = 16
GLA_GATE_TAU = 16.0
PEER_HEADS = 8
PEER_NKEYS = 128
PEER_HALF = 128
PEER_TOPK = 16
N_MOD = 6
EPS = 1e-6
W_IN_MAIN = 3 * CONV_DIM + 2 * GLA_DK + 2 * GLA_DV + 2 * D_MODEL
OFF_BG, OFF_CG, OFF_HIN = 0, 512, 1024
OFF_Q, OFF_K, OFF_V, OFF_R, OFF_GA, OFF_GB = 1536, 2048, 2560, 3584, 4608, 5632

LANES = 128
SUBLANES = 8
VMEM_LIMIT_BYTES = 58 * 1024 * 1024

GLA_CHUNK = 128
MIX_TB_PROMPT = 512
MIX_SEQ_SAMPLE = 16
PEER_TB = 512
PEER_EC = 2048
PEER_I1 = PEER_EC // PEER_NKEYS
GELU_C0 = math.sqrt(2.0 / math.pi)
GELU_C1 = 0.044715


def _dot(a, b):
    return jnp.dot(a, b, preferred_element_type=F32)


def _dot_nt(a, b):
    return lax.dot_general(a, b, (((1,), (1,)), ((), ())), preferred_element_type=F32)


def _dot_tn(a, b):
    return lax.dot_general(a, b, (((0,), (0,)), ((), ())), preferred_element_type=F32)


def _split2(x):
    p0 = x.astype(BF16)
    return p0, (x - p0.astype(F32)).astype(BF16)


def _rms(x, w):
    return x * lax.rsqrt(jnp.mean(x * x, axis=-1, keepdims=True) + EPS) * w


def _sigmoid(x):
    return 1.0 / (1.0 + jnp.exp(-x))


def _resident(shape, index_map):
    return pl.BlockSpec(shape, index_map, pipeline_mode=pl.Buffered(1))


def _rows_from_segments(seg_vals, seg_len):
    nseg, _, d = seg_vals.shape
    return jnp.broadcast_to(seg_vals, (nseg, seg_len, d)).reshape(nseg * seg_len, d)


def _ada_kernel(c_ref, w_ref, b_ref, o_ref):
    c = c_ref[...]
    a = (c * _sigmoid(c)).astype(BF16)
    o_ref[...] = _dot(a, w_ref[...].astype(BF16)) + b_ref[...]


def _ada_call(c_all, w_ada, b_ada):
    rows = c_all.shape[0]
    ncols = w_ada.shape[1]
    bn = 1536
    return pl.pallas_call(
        _ada_kernel,
        grid=(ncols // bn,),
        in_specs=[
            pl.BlockSpec((rows, D_MODEL), lambda j: (0, 0)),
            pl.BlockSpec((D_MODEL, bn), lambda j: (0, j)),
            pl.BlockSpec((1, bn), lambda j: (0, j)),
        ],
        out_specs=pl.BlockSpec((rows, bn), lambda j: (0, j)),
        out_shape=jax.ShapeDtypeStruct((rows, ncols), F32),
        compiler_params=pltpu.CompilerParams(
            dimension_semantics=("arbitrary",), vmem_limit_bytes=VMEM_LIMIT_BYTES),
        name="ada_mod",
    )(c_all, w_ada, b_ada.reshape(1, ncols))


def _gla_levels(seg_len):
    return [b for b in (1, 2, 4, 8, 16, 32, 64) if 2 * b <= seg_len]


def _gla_consts(chunk, seg_len):
    t = np.arange(chunk)
    seg = t // seg_len
    tri = (t[None, :] <= t[:, None]) & (seg[None, :] == seg[:, None])
    masks = []
    for b in _gla_levels(seg_len):
        parent = t // (2 * b)
        upper = (t % (2 * b)) >= b
        masks.append((parent[:, None] == parent[None, :]) & upper[:, None] & ~upper[None, :])
    masks.append(np.eye(chunk, dtype=bool))
    segones = np.zeros((chunk, LANES), np.float32)
    segones[t, seg] = 1.0
    return (jnp.asarray(tri.astype(np.float32), BF16),
            jnp.asarray(np.stack(masks).astype(np.float32)),
            jnp.asarray(segones, BF16))


def _group_row(x, group, row):
    n, d = x.shape
    if group >= SUBLANES:
        picked = x.reshape(n // group, group, d)[:, row:row + 1, :]
        return jnp.broadcast_to(picked, (n // group, group, d)).reshape(n, d)
    pos = lax.broadcasted_iota(jnp.int32, (n, 1), 0) % group
    out = x
    for p in range(group):
        if p != row:
            out = jnp.where(pos == p, pltpu.roll(x, (p - row) % n, 0), out)
    return out


def _gla_chunk(q, k, v, la, a_all, masks, segones, states, seg_len):
    c = q.shape[0]
    nseg = c // seg_len
    levels = _gla_levels(seg_len)
    nl = len(levels)
    pieces = _split2(la)
    cum = sum(_dot(a_all, p) for p in pieces)
    cum_last_t = sum(_dot_tn(p, segones) for p in pieces)
    decay_cols = jnp.exp(cum_last_t)
    e_cum = jnp.exp(cum)
    e_last = jnp.exp(_group_row(cum, seg_len, seg_len - 1) - cum)
    e_level = [jnp.exp(-jnp.abs(cum - _group_row(cum, 2 * b, b - 1))) for b in levels]
    outs = []
    new_states = [[None] * GLA_HEADS for _ in range(nseg)]
    for h in range(GLA_HEADS):
        ks = slice(h * GLA_HEAD_K, (h + 1) * GLA_HEAD_K)
        vs = slice(h * GLA_HEAD_V, (h + 1) * GLA_HEAD_V)
        qh, kh = q[:, ks], k[:, ks]
        vh = v[:, vs].astype(BF16)
        scores = _dot_nt(qh.astype(BF16), kh.astype(BF16)) * masks[nl]
        for li in range(nl):
            eb = e_level[li][:, ks]
            scores = scores + _dot_nt((qh * eb).astype(BF16), (kh * eb).astype(BF16)) * masks[li]
        o_h = _dot(scores.astype(BF16), vh)
        qt = qh * e_cum[:, ks]
        kt = kh * e_last[:, ks]
        vf = v[:, vs]
        inter = []
        for s in range(nseg):
            rows = slice(s * seg_len, (s + 1) * seg_len)
            st = states[s][h]
            inter.append(_dot(qt[rows].astype(BF16), st.astype(BF16)))
            new_states[s][h] = decay_cols[ks, s:s + 1] * st + _dot_tn(
                kt[rows].astype(BF16), vf[rows].astype(BF16))
        o_h = o_h + (inter[0] if nseg == 1 else jnp.concatenate(inter, axis=0))
        outs.append(o_h)
    return jnp.concatenate(outs, axis=1), new_states


def _mixer_kernel(cfg, x_ref, mod_ref, cst_ref, gst_ref, npre_ref, npost_ref, win_ref, wal_ref,
                  convw_ref, convb_ref, walpha_ref, balpha_ref, gnorm_ref, wpa_ref, wpb_ref,
                  wout_ref, aall_ref, masks_ref, segones_ref,
                  xo_ref, cnew_ref, gnew_ref, proj_ref, o_ref, carry_ref, state_ref):
    tb, nseq, carried = cfg["tb"], cfg["nseq"], cfg["carried"]
    lseq = tb // nseq
    chunk = min(GLA_CHUNK, tb)
    seg_len = min(lseq, chunk)
    step = pl.program_id(1)

    if carried:
        @pl.when(step == 0)
        def _():
            carry_ref[...] = jnp.zeros_like(carry_ref)
            state_ref[...] = jnp.zeros_like(state_ref)

    x = x_ref[...]
    mod = mod_ref[...]
    sh = _rows_from_segments(mod[:, :, 0:D_MODEL], lseq)
    sc = _rows_from_segments(mod[:, :, D_MODEL:2 * D_MODEL], lseq)
    gate = _rows_from_segments(mod[:, :, 2 * D_MODEL:3 * D_MODEL], lseq)
    h = (_rms(x, npre_ref[...]) * (1.0 + sc) + sh).astype(BF16)
    proj_ref[...] = _dot(h, win_ref[...])
    alow = _dot(h, wal_ref[...])

    u = proj_ref[:, OFF_CG:OFF_CG + CONV_DIM] * proj_ref[:, OFF_HIN:OFF_HIN + CONV_DIM]
    prev = carry_ref[...] if carried else cst_ref[...]
    prev0 = _rows_from_segments(prev[:, 0:1, :], lseq)
    prev1 = _rows_from_segments(prev[:, 1:2, :], lseq)
    pos = lax.broadcasted_iota(jnp.int32, (tb, 1), 0) % lseq
    u1 = jnp.where(pos == 0, prev1, pltpu.roll(u, 1, 0))
    u2 = jnp.where(pos == 0, prev0, jnp.where(pos == 1, prev1, pltpu.roll(u, 2, 0)))
    cw = convw_ref[...]
    y_conv = convb_ref[...] + cw[2:3, :] * u + cw[0:1, :] * u2 + cw[1:2, :] * u1
    new_buf = u.reshape(nseq, lseq, CONV_DIM)[:, lseq - 2:lseq, :]
    if carried:
        carry_ref[...] = new_buf
    cnew_ref[...] = new_buf
    branch_a = _dot((proj_ref[:, OFF_BG:OFF_BG + CONV_DIM] * y_conv).astype(BF16), wpa_ref[...])

    z = _dot(alow.astype(BF16), walpha_ref[...]) + balpha_ref[...]
    log_a = (jnp.minimum(z, 0.0) - jnp.log1p(jnp.exp(-jnp.abs(z)))) * (1.0 / GLA_GATE_TAU)
    a_all = aall_ref[...]
    masks = [masks_ref[i] for i in range(masks_ref.shape[0])]
    segones = segones_ref[...]
    nseg = chunk // seg_len
    for c in range(tb // chunk):
        rows = slice(c * chunk, (c + 1) * chunk)
        if carried:
            states = [[state_ref[hh] for hh in range(GLA_HEADS)]]
        else:
            states = [[gst_ref[c * nseg + s, hh] for hh in range(GLA_HEADS)] for s in range(nseg)]
        o_c, new_states = _gla_chunk(
            proj_ref[rows, OFF_Q:OFF_Q + GLA_DK] * (GLA_HEAD_K ** -0.5),
            proj_ref[rows, OFF_K:OFF_K + GLA_DK],
            proj_ref[rows, OFF_V:OFF_V + GLA_DV],
            log_a[rows], a_all, masks, segones, states, seg_len)
        o_ref[rows, :] = o_c
        for s in range(nseg):
            for hh in range(GLA_HEADS):
                if carried:
                    state_ref[hh] = new_states[s][hh]
                else:
                    gnew_ref[c * nseg + s, hh] = new_states[s][hh]
    if carried:
        gnew_ref[0] = state_ref[...]

    gn = gnorm_ref[...]
    normed = []
    for hh in range(GLA_HEADS):
        vs = slice(hh * GLA_HEAD_V, (hh + 1) * GLA_HEAD_V)
        normed.append(_rms(o_ref[:, vs], gn[:, vs]))
    r = proj_ref[:, OFF_R:OFF_R + GLA_DV]
    ob = jnp.concatenate(normed, axis=1) * (r * _sigmoid(r))
    branch_b = _dot(ob.astype(BF16), wpb_ref[...])

    merged = (_sigmoid(proj_ref[:, OFF_GA:OFF_GA + D_MODEL]) * branch_a
              + _sigmoid(proj_ref[:, OFF_GB:OFF_GB + D_MODEL]) * branch_b)
    m = _dot(merged.astype(BF16), wout_ref[...])
    xo_ref[...] = x + gate * _rms(m, npost_ref[...])


def _mixer_call(x2d, mod3, conv_state, gla_state, weights, *, nseq_total, seq_len, carried):
    if carried:
        tb, nseq = MIX_TB_PROMPT, 1
        grid = (nseq_total, seq_len // tb)
        x_map = lambda b, c: (b * (seq_len // tb) + c, 0)
    else:
        nseq = MIX_SEQ_SAMPLE
        tb = nseq * seq_len
        grid = (nseq_total // nseq, 1)
        x_map = lambda b, c: (b, 0)
    chunk = min(GLA_CHUNK, tb)
    seg_len = min(tb // nseq, chunk)
    a_all, masks, segones = _gla_consts(chunk, seg_len)
    cfg = dict(tb=tb, nseq=nseq, carried=carried)
    const2 = lambda b, c: (0, 0)
    const3 = lambda b, c: (0, 0, 0)
    (npre, npost, w_in_main, w_al, conv_w, conv_b, w_alpha, b_alpha, gnorm, wpa, wpb, wout) = weights
    n_tok = x2d.shape[0]
    state_spec = pl.BlockSpec(
        (nseq, GLA_HEADS, GLA_HEAD_K, GLA_HEAD_V), lambda b, c: (b, 0, 0, 0),
        **({} if carried else dict(pipeline_mode=pl.Buffered(1))))
    in_specs = [
        pl.BlockSpec((tb, D_MODEL), x_map),
        pl.BlockSpec((nseq, 1, 3 * D_MODEL), lambda b, c: (b, 0, 0)),
        pl.BlockSpec((nseq, CONV_W - 1, CONV_DIM), lambda b, c: (b, 0, 0)),
        state_spec,
        _resident((1, D_MODEL), const2),
        _resident((1, D_MODEL), const2),
        _resident((D_MODEL, W_IN_MAIN), const2),
        _resident((D_MODEL, LANES), const2),
        _resident((CONV_W, CONV_DIM), const2),
        _resident((1, CONV_DIM), const2),
        _resident((LANES, GLA_DK), const2),
        _resident((1, GLA_DK), const2),
        _resident((1, GLA_DV), const2),
        _resident((CONV_DIM, D_MODEL), const2),
        _resident((GLA_DV, D_MODEL), const2),
        _resident((D_MODEL, D_MODEL), const2),
        _resident(a_all.shape, const2),
        _resident(masks.shape, const3),
        _resident(segones.shape, const2),
    ]
    out_specs = [
        pl.BlockSpec((tb, D_MODEL), x_map),
        pl.BlockSpec((nseq, CONV_W - 1, CONV_DIM), lambda b, c: (b, 0, 0)),
        state_spec,
    ]
    out_shape = [
        jax.ShapeDtypeStruct((n_tok, D_MODEL), F32),
        jax.ShapeDtypeStruct((nseq_total, CONV_W - 1, CONV_DIM), F32),
        jax.ShapeDtypeStruct((nseq_total, GLA_HEADS, GLA_HEAD_K, GLA_HEAD_V), F32),
    ]
    scratch = [
        pltpu.VMEM((tb, W_IN_MAIN), F32),
        pltpu.VMEM((tb, GLA_DV), F32),
        pltpu.VMEM((1, CONV_W - 1, CONV_DIM), F32),
        pltpu.VMEM((GLA_HEADS, GLA_HEAD_K, GLA_HEAD_V), F32),
    ]
    return pl.pallas_call(
        functools.partial(_mixer_kernel, cfg),
        grid=grid,
        in_specs=in_specs,
        out_specs=out_specs,
        out_shape=out_shape,
        scratch_shapes=scratch,
        compiler_params=pltpu.CompilerParams(
            dimension_semantics=("arbitrary", "arbitrary"), vmem_limit_bytes=VMEM_LIMIT_BYTES),
        name="mixer_prompt" if carried else "mixer_sample",
    )(x2d, mod3, conv_state, gla_state, npre, npost, w_in_main, w_al, conv_w, conv_b, w_alpha,
      b_alpha, gnorm, wpa, wpb, wout, a_all, masks, segones)


def _wc_kernel(sk_ref, wq_ref, o_ref):
    o_ref[0] = lax.dot_general(sk_ref[0], wq_ref[...], (((1,), (1,)), ((), ())),
                               precision=lax.Precision.HIGHEST,
                               preferred_element_type=F32).astype(BF16)


def _wc_call(subkeys, wq):
    nhp = PEER_HEADS * 2
    out = pl.pallas_call(
        _wc_kernel,
        grid=(nhp,),
        in_specs=[
            pl.BlockSpec((1, PEER_NKEYS, PEER_HALF), lambda i: (i, 0, 0)),
            pl.BlockSpec((D_MODEL, PEER_HALF), lambda i: (0, i)),
        ],
        out_specs=pl.BlockSpec((1, PEER_NKEYS, D_MODEL), lambda i: (i, 0, 0)),
        out_shape=jax.ShapeDtypeStruct((nhp, PEER_NKEYS, D_MODEL), BF16),
        compiler_params=pltpu.CompilerParams(
            dimension_semantics=("arbitrary",), vmem_limit_bytes=VMEM_LIMIT_BYTES),
        name="peer_wc",
    )(subkeys.reshape(nhp, PEER_NKEYS, PEER_HALF), wq)
    return out.reshape(PEER_HEADS, 2, PEER_NKEYS, D_MODEL).transpose(1, 0, 2, 3).reshape(
        2 * PEER_HEADS * PEER_NKEYS, D_MODEL)


def _cand_pairs():
    n = PEER_TOPK + 1
    return [(a, b) for a in range(n) for b in range(n) if (a + 1) * (b + 1) <= n]


def _tree_reduce(op, xs):
    xs = list(xs)
    while len(xs) > 1:
        xs = [op(xs[i], xs[i + 1]) if i + 1 < len(xs) else xs[i] for i in range(0, len(xs), 2)]
    return xs[0]


def _desc_distinct(vals, count):
    out = [_tree_reduce(jnp.maximum, vals)]
    for _ in range(count - 1):
        m = out[-1]
        out.append(_tree_reduce(jnp.maximum, [jnp.where(v < m, v, -jnp.inf) for v in vals]))
    return out


def _store_gate_factors(st_ref, e1_ref, e2_ref, lg, h, max1, max2, half_inv_z_row):
    s1 = st_ref[lg, h * PEER_NKEYS:(h + 1) * PEER_NKEYS, :]
    row2 = (PEER_HEADS + h) * PEER_NKEYS
    s2 = st_ref[lg, row2:row2 + PEER_NKEYS, :]
    e1_ref[lg, h] = jnp.exp(s1 - max1) * half_inv_z_row
    e2_ref[lg, h] = jnp.exp(s2 - max2)


def _peer_select(st_ref, c1_ref, e1_ref, e2_ref, lg):
    nk = PEER_TOPK + 1
    tops = [[None] * PEER_HEADS for _ in range(2)]
    bad = jnp.zeros((1, LANES), F32)
    for p in range(2):
        for h in range(PEER_HEADS):
            r0 = (p * PEER_HEADS + h) * PEER_NKEYS
            s = st_ref[lg, r0:r0 + PEER_NKEYS, :]
            out = [jnp.max(s, axis=0, keepdims=True)]
            for _ in range(nk - 1):
                out.append(jnp.max(jnp.where(s < out[-1], s, -jnp.inf), axis=0, keepdims=True))
            tops[p][h] = out
            n_top = jnp.sum(jnp.where(s >= out[-1], 1.0, 0.0), axis=0, keepdims=True)
            bad = jnp.maximum(bad, jnp.where(n_top != float(nk), 1.0, 0.0))
    packed = [[jnp.concatenate([tops[p][h][k] for h in range(PEER_HEADS)], axis=0)
               for k in range(nk)] for p in range(2)]
    cands = [packed[0][a] + packed[1][b] for (a, b) in _cand_pairs()]
    t = _desc_distinct(cands, nk)
    n_top = _tree_reduce(jnp.add, [jnp.where(c >= t[-1], 1.0, 0.0) for c in cands])
    bad = jnp.maximum(bad, jnp.max(jnp.where(n_top != float(nk), 1.0, 0.0), axis=0, keepdims=True))
    tau = 0.5 * (t[PEER_TOPK - 1] + t[PEER_TOPK])
    z = jnp.ones_like(t[0])
    for k in range(1, PEER_TOPK):
        z = z + jnp.exp(t[k] - t[0])
    half_inv_z = 0.5 / z
    for h in range(PEER_HEADS):
        s1 = st_ref[lg, h * PEER_NKEYS:(h + 1) * PEER_NKEYS, :]
        c1_ref[lg, h] = tau[h:h + 1, :] - s1
        _store_gate_factors(st_ref, e1_ref, e2_ref, lg, h, tops[0][h][0], tops[1][h][0],
                            half_inv_z[h:h + 1, :])

    @pl.when(jnp.max(bad) > 0.0)
    def _():
        for h in range(PEER_HEADS):
            row2 = (PEER_HEADS + h) * PEER_NKEYS
            s2 = st_ref[lg, row2:row2 + PEER_NKEYS, :]
            c1 = c1_ref[lg, h]
            n_sel = jnp.zeros((PEER_NKEYS, LANES), F32)
            for k in range(nk):
                v = tops[1][h][k]
                mult = jnp.sum(jnp.where(s2 == v, 1.0, 0.0), axis=0, keepdims=True)
                n_sel = n_sel + jnp.where(v >= c1, mult, 0.0)
            total = jnp.sum(n_sel, axis=0, keepdims=True)

            @pl.when(jnp.max(jnp.where(total != float(PEER_TOPK), 1.0, 0.0)) > 0.0)
            def _(h=h):
                _peer_select_exact(st_ref, c1_ref, e1_ref, e2_ref, lg, h)


def _peer_select_exact(st_ref, c1_ref, e1_ref, e2_ref, lg, h):
    n_pairs = PEER_TOPK * PEER_TOPK
    row1 = h * PEER_NKEYS
    row2 = (PEER_HEADS + h) * PEER_NKEYS
    s1 = st_ref[lg, row1:row1 + PEER_NKEYS, :]
    s2 = st_ref[lg, row2:row2 + PEER_NKEYS, :]
    key_id = lax.broadcasted_iota(jnp.int32, (1, PEER_NKEYS, LANES), 1).astype(F32)
    slot_id = lax.broadcasted_iota(jnp.int32, (1, PEER_TOPK, 1), 1)

    def pick(k, state):
        rem, rank, vals = state
        m = jnp.max(rem, axis=1, keepdims=True)
        first = jnp.min(jnp.where(rem == m, key_id, float(PEER_NKEYS)), axis=1, keepdims=True)
        hit = key_id == first
        return (jnp.where(hit, -jnp.inf, rem),
                jnp.where(hit, lax.convert_element_type(k, F32), rank),
                jnp.where(slot_id == k, m, vals))
    _, rank, vals = lax.fori_loop(
        0, PEER_TOPK, pick,
        (jnp.stack([s1, s2]), jnp.full((2, PEER_NKEYS, LANES), float(PEER_TOPK), F32),
         jnp.zeros((2, PEER_TOPK, LANES), F32)))

    cand = (vals[0][:, None, :] + vals[1][None, :, :]).reshape(n_pairs, LANES)
    flat = lax.broadcasted_iota(jnp.int32, (n_pairs, 1), 0).astype(F32)

    def pick_pair(k, state):
        rem, taken, z, t0 = state
        m = jnp.max(rem, axis=0, keepdims=True)
        first = jnp.min(jnp.where(rem == m, flat, float(n_pairs)), axis=0, keepdims=True)
        hit = flat == first
        taken = taken + jnp.sum(jnp.where(hit, 1.0, 0.0).reshape(PEER_TOPK, PEER_TOPK, LANES), axis=1)
        t0 = jnp.where(k == 0, m, t0)
        return jnp.where(hit, -jnp.inf, rem), taken, z + jnp.exp(m - t0), t0
    zeros = jnp.zeros((1, LANES), F32)
    _, taken, z, _ = lax.fori_loop(0, PEER_TOPK, pick_pair,
                                   (cand, jnp.zeros((PEER_TOPK, LANES), F32), zeros, zeros))
    n1 = jnp.zeros((PEER_NKEYS, LANES), F32)
    for a in range(PEER_TOPK):
        n1 = n1 + jnp.where(rank[0] == float(a), taken[a:a + 1, :], 0.0)
    e1_ref[lg, h] = jnp.exp(s1 - vals[0][0:1, :]) * (0.5 / z)
    e2_ref[lg, h] = jnp.exp(s2 - vals[1][0:1, :])
    c1_ref[lg, h] = 0.5 - n1
    st_ref[lg, row2:row2 + PEER_NKEYS, :] = -rank[1]


def _peer_kernel(cfg, x_ref, mod_ref, npre_ref, npost_ref, wct_ref, u_ref, vt_ref, y_ref,
                 h2_ref, st_ref, c1_ref, e1_ref, e2_ref, a_ref, w_ref, acc_ref):
    tb, nseq = cfg["tb"], cfg["nseq"]
    lseq = tb // nseq
    e = pl.program_id(1)
    n_e = pl.num_programs(1)
    n_lg = tb // LANES

    def lane_group_rows(seg_vals, lg):
        if nseq == 1:
            return _rows_from_segments(seg_vals, LANES)
        per_lg = nseq // n_lg
        return _rows_from_segments(seg_vals[lg * per_lg:(lg + 1) * per_lg], lseq)

    @pl.when(e == 0)
    def _():
        mod = mod_ref[...]
        for lg in range(n_lg):
            sh = lane_group_rows(mod[:, :, 0:D_MODEL], lg)
            sc = lane_group_rows(mod[:, :, D_MODEL:2 * D_MODEL], lg)
            h2 = _rms(x_ref[lg * LANES:(lg + 1) * LANES, :], npre_ref[...]) * (1.0 + sc) + sh
            h2_ref[:, lg * LANES:(lg + 1) * LANES] = h2.T.astype(BF16)
        score_rows = 2 * PEER_NKEYS
        for r in range(0, 2 * PEER_HEADS * PEER_NKEYS, score_rows):
            st = _dot(wct_ref[r:r + score_rows, :], h2_ref[...])
            for lg in range(n_lg):
                st_ref[lg, r:r + score_rows, :] = st[:, lg * LANES:(lg + 1) * LANES]

        def select(lg, carry):
            _peer_select(st_ref, c1_ref, e1_ref, e2_ref, lg)
            return carry
        lax.fori_loop(0, n_lg, select, 0)
        acc_ref[...] = jnp.zeros_like(acc_ref)

    a_all = _dot(u_ref[...], h2_ref[...])
    for lg in range(n_lg):
        a_ref[lg] = a_all[:, lg * LANES:(lg + 1) * LANES]
    i1_tile = pl.ds(pl.multiple_of(e * PEER_I1, SUBLANES), PEER_I1)

    def gate_lane_group(lg, carry):
        lanes = pl.ds(pl.multiple_of(lg * LANES, LANES), LANES)
        c1t = [c1_ref[lg, h, i1_tile, :] for h in range(PEER_HEADS)]
        e1t = [e1_ref[lg, h, i1_tile, :] for h in range(PEER_HEADS)]
        for i1 in range(PEER_I1):
            g = jnp.zeros((PEER_NKEYS, LANES), F32)
            for h in range(PEER_HEADS):
                row2 = (PEER_HEADS + h) * PEER_NKEYS
                s2 = st_ref[lg, row2:row2 + PEER_NKEYS, :]
                p = e2_ref[lg, h] * e1t[h][i1:i1 + 1, :]
                g = g + jnp.where(s2 >= c1t[h][i1:i1 + 1, :], p, 0.0)
            rows = slice(i1 * PEER_NKEYS, (i1 + 1) * PEER_NKEYS)
            a = a_ref[lg, rows, :]
            act2 = a + a * jnp.tanh(a * (GELU_C0 + (GELU_C0 * GELU_C1) * (a * a)))
            w_ref[rows, lanes] = (act2 * g).astype(BF16)
        return carry
    lax.fori_loop(0, n_lg, gate_lane_group, 0)
    acc_ref[...] += _dot(vt_ref[...], w_ref[...])

    @pl.when(e == n_e - 1)
    def _():
        mod = mod_ref[...]
        for lg in range(n_lg):
            rows = slice(lg * LANES, (lg + 1) * LANES)
            gate = lane_group_rows(mod[:, :, 2 * D_MODEL:3 * D_MODEL], lg)
            out = acc_ref[:, rows].T
            y_ref[rows, :] = x_ref[rows, :] + gate * _rms(out, npost_ref[...])


def _peer_call(x2d, mod3, npre, npost, wct, u_bf, vt_bf, *, nseq_total, seq_len):
    n_tok = x2d.shape[0]
    tb = PEER_TB
    if seq_len >= tb:
        nseq = 1
        mod_map = lambda i, e: (i // (seq_len // tb), 0, 1)
    else:
        nseq = tb // seq_len
        mod_map = lambda i, e: (i, 0, 1)
    n_exp = u_bf.shape[0]
    assert n_exp % PEER_EC == 0 and PEER_I1 % SUBLANES == 0
    cfg = dict(tb=tb, nseq=nseq)
    const2 = lambda i, e: (0, 0)
    return pl.pallas_call(
        functools.partial(_peer_kernel, cfg),
        grid=(n_tok // tb, n_exp // PEER_EC),
        in_specs=[
            pl.BlockSpec((tb, D_MODEL), lambda i, e: (i, 0), pipeline_mode=pl.Buffered(1)),
            pl.BlockSpec((nseq, 1, 3 * D_MODEL), mod_map, pipeline_mode=pl.Buffered(1)),
            _resident((1, D_MODEL), const2),
            _resident((1, D_MODEL), const2),
            _resident(wct.shape, const2),
            pl.BlockSpec((PEER_EC, D_MODEL), lambda i, e: (e, 0)),
            pl.BlockSpec((D_MODEL, PEER_EC), lambda i, e: (0, e)),
        ],
        out_specs=pl.BlockSpec((tb, D_MODEL), lambda i, e: (i, 0)),
        out_shape=jax.ShapeDtypeStruct((n_tok, D_MODEL), F32),
        scratch_shapes=[
            pltpu.VMEM((D_MODEL, tb), BF16),
            pltpu.VMEM((tb // LANES, 2 * PEER_HEADS * PEER_NKEYS, LANES), F32),
            pltpu.VMEM((tb // LANES, PEER_HEADS, PEER_NKEYS, LANES), F32),
            pltpu.VMEM((tb // LANES, PEER_HEADS, PEER_NKEYS, LANES), F32),
            pltpu.VMEM((tb // LANES, PEER_HEADS, PEER_NKEYS, LANES), F32),
            pltpu.VMEM((tb // LANES, PEER_EC, LANES), F32),
            pltpu.VMEM((PEER_EC, tb), BF16),
            pltpu.VMEM((D_MODEL, tb), F32),
        ],
        compiler_params=pltpu.CompilerParams(
            dimension_semantics=("arbitrary", "arbitrary"), vmem_limit_bytes=VMEM_LIMIT_BYTES),
        name="peer_prompt" if nseq == 1 else "peer_sample",
    )(x2d, mod3, npre, npost, wct, u_bf, vt_bf)


def kernel(x_prompt, x_sample, c_prompt, c_sample, state_conv, state_gla, norm_mix_pre, norm_mix_post, norm_ffn_pre, norm_ffn_post, w_ada, b_ada, w_in, conv_w, conv_b, w_alpha, b_alpha, gla_norm_w, w_proj_a, w_proj_b, w_out, peer_wq, peer_subkeys, peer_u, peer_v):
    depth = w_in.shape[0]
    assert depth == 1, "single-layer trunk"
    n_p, t_p, _ = x_prompt.shape
    n_s, t_s, _ = x_sample.shape
    l = 0

    c_all = jnp.concatenate([c_prompt, c_sample], axis=0)
    mod = _ada_call(c_all, w_ada[l], b_ada[l])
    mod_p = mod[:n_p].reshape(n_p, 1, N_MOD * D_MODEL)
    mod_s = mod[n_p:].reshape(n_s, 1, N_MOD * D_MODEL)

    w_in_main = w_in[l][:, :W_IN_MAIN].astype(BF16)
    w_al = jnp.pad(w_in[l][:, W_IN_MAIN:], ((0, 0), (0, LANES - GLA_GATE_RANK))).astype(BF16)
    w_alpha_p = jnp.pad(w_alpha[l], ((0, LANES - GLA_GATE_RANK), (0, 0))).astype(BF16)
    row = lambda a: a.reshape(1, -1)
    mixer_w = (row(norm_mix_pre[l]), row(norm_mix_post[l]), w_in_main, w_al, conv_w[l], row(conv_b[l]),
               w_alpha_p, row(b_alpha[l]), row(gla_norm_w[l]), w_proj_a[l].astype(BF16),
               w_proj_b[l].astype(BF16), w_out[l].astype(BF16))

    zeros_conv = jnp.zeros((n_p, CONV_W - 1, CONV_DIM), state_conv.dtype)
    zeros_gla = jnp.zeros((n_p, GLA_HEADS, GLA_HEAD_K, GLA_HEAD_V), state_gla.dtype)
    x1_p, conv_p, gla_p = _mixer_call(
        x_prompt.reshape(n_p * t_p, D_MODEL), mod_p, zeros_conv, zeros_gla, mixer_w,
        nseq_total=n_p, seq_len=t_p, carried=True)
    x1_s, conv_s, gla_s = _mixer_call(
        x_sample.reshape(n_s * t_s, D_MODEL), mod_s, state_conv[l], state_gla[l], mixer_w,
        nseq_total=n_s, seq_len=t_s, carried=False)

    wct = _wc_call(peer_subkeys[l], peer_wq[l])
    u_bf = peer_u[l].astype(BF16)
    vt_bf = peer_v[l].astype(BF16).T
    npre2, npost2 = row(norm_ffn_pre[l]), row(norm_ffn_post[l])
    y_p = _peer_call(x1_p, mod_p, npre2, npost2, wct, u_bf, vt_bf, nseq_total=n_p, seq_len=t_p)
    y_s = _peer_call(x1_s, mod_s, npre2, npost2, wct, u_bf, vt_bf, nseq_total=n_s, seq_len=t_s)

    return (y_p.reshape(n_p, t_p, D_MODEL), y_s.reshape(n_s, t_s, D_MODEL),
            conv_p[None], gla_p[None], conv_s[None], gla_s[None])
```

```python
import functools
import math

import jax
import jax.numpy as jnp
import numpy as np
from jax import lax
from jax.experimental import pallas as pl
from jax.experimental.pallas import tpu as pltpu

F32 = jnp.float32
BF16 = jnp.bfloat16

D_MODEL = 1024
CONV_DIM = 512
CONV_W = 3
GLA_HEADS = 4
GLA_DK = 512
GLA_DV = 1024
GLA_HEAD_K = 128
GLA_HEAD_V = 256
GLA_GATE_RANK = 16
GLA_GATE_TAU = 16.0
PEER_HEADS = 8
PEER_NKEYS = 128
PEER_HALF = 128
PEER_TOPK = 16
N_MOD = 6
EPS = 1e-6
W_IN_MAIN = 3 * CONV_DIM + 2 * GLA_DK + 2 * GLA_DV + 2 * D_MODEL
OFF_BG, OFF_CG, OFF_HIN = 0, 512, 1024
OFF_Q, OFF_K, OFF_V, OFF_R, OFF_GA, OFF_GB = 1536, 2048, 2560, 3584, 4608, 5632

LANES = 128
SUBLANES = 8
VMEM_LIMIT_BYTES = 58 * 1024 * 1024

GLA_CHUNK = 128
MIX_TB_PROMPT = 512
MIX_SEQ_SAMPLE = 16
PEER_TB = 512
PEER_EC = 2048
PEER_I1 = PEER_EC // PEER_NKEYS
GELU_C0 = math.sqrt(2.0 / math.pi)
GELU_C1 = 0.044715


def _dot(a, b):
    return jnp.dot(a, b, preferred_element_type=F32)


def _dot_nt(a, b):
    return lax.dot_general(a, b, (((1,), (1,)), ((), ())), preferred_element_type=F32)


def _dot_tn(a, b):
    return lax.dot_general(a, b, (((0,), (0,)), ((), ())), preferred_element_type=F32)


def _split2(x):
    p0 = x.astype(BF16)
    return p0, (x - p0.astype(F32)).astype(BF16)


def _rms(x, w):
    return x * lax.rsqrt(jnp.mean(x * x, axis=-1, keepdims=True) + EPS) * w


def _sigmoid(x):
    return 1.0 / (1.0 + jnp.exp(-x))


def _resident(shape, index_map):
    return pl.BlockSpec(shape, index_map, pipeline_mode=pl.Buffered(1))


def _rows_from_segments(seg_vals, seg_len):
    nseg, _, d = seg_vals.shape
    return jnp.broadcast_to(seg_vals, (nseg, seg_len, d)).reshape(nseg * seg_len, d)


def _ada_kernel(c_ref, w_ref, b_ref, o_ref):
    c = c_ref[...]
    a = (c * _sigmoid(c)).astype(BF16)
    o_ref[...] = _dot(a, w_ref[...].astype(BF16)) + b_ref[...]


def _ada_call(c_all, w_ada, b_ada):
    rows = c_all.shape[0]
    ncols = w_ada.shape[1]
    bn = 1536
    return pl.pallas_call(
        _ada_kernel,
        grid=(ncols // bn,),
        in_specs=[
            pl.BlockSpec((rows, D_MODEL), lambda j: (0, 0)),
            pl.BlockSpec((D_MODEL, bn), lambda j: (0, j)),
            pl.BlockSpec((1, bn), lambda j: (0, j)),
        ],
        out_specs=pl.BlockSpec((rows, bn), lambda j: (0, j)),
        out_shape=jax.ShapeDtypeStruct((rows, ncols), F32),
        compiler_params=pltpu.CompilerParams(
            dimension_semantics=("arbitrary",), vmem_limit_bytes=VMEM_LIMIT_BYTES),
        name="ada_mod",
    )(c_all, w_ada, b_ada.reshape(1, ncols))


def _gla_levels(seg_len):
    return [b for b in (1, 2, 4, 8, 16, 32, 64) if 2 * b <= seg_len]


def _gla_consts(chunk, seg_len):
    t = np.arange(chunk)
    seg = t // seg_len
    tri = (t[None, :] <= t[:, None]) & (seg[None, :] == seg[:, None])
    masks = []
    for b in _gla_levels(seg_len):
        parent = t // (2 * b)
        upper = (t % (2 * b)) >= b
        masks.append((parent[:, None] == parent[None, :]) & upper[:, None] & ~upper[None, :])
    masks.append(np.eye(chunk, dtype=bool))
    segones = np.zeros((chunk, LANES), np.float32)
    segones[t, seg] = 1.0
    return (jnp.asarray(tri.astype(np.float32), BF16),
            jnp.asarray(np.stack(masks).astype(np.float32)),
            jnp.asarray(segones, BF16))


def _group_row(x, group, row):
    n, d = x.shape
    if group >= SUBLANES:
        picked = x.reshape(n // group, group, d)[:, row:row + 1, :]
        return jnp.broadcast_to(picked, (n // group, group, d)).reshape(n, d)
    pos = lax.broadcasted_iota(jnp.int32, (n, 1), 0) % group
    out = x
    for p in range(group):
        if p != row:
            out = jnp.where(pos == p, pltpu.roll(x, (p - row) % n, 0), out)
    return out


def _gla_chunk(q, k, v, la, a_all, masks, segones, states, seg_len):
    c = q.shape[0]
    nseg = c // seg_len
    levels = _gla_levels(seg_len)
    nl = len(levels)
    pieces = _split2(la)
    cum = sum(_dot(a_all, p) for p in pieces)
    cum_last_t = sum(_dot_tn(p, segones) for p in pieces)
    decay_cols = jnp.exp(cum_last_t)
    e_cum = jnp.exp(cum)
    e_last = jnp.exp(_group_row(cum, seg_len, seg_len - 1) - cum)
    e_level = [jnp.exp(-jnp.abs(cum - _group_row(cum, 2 * b, b - 1))) for b in levels]
    outs = []
    new_states = [[None] * GLA_HEADS for _ in range(nseg)]
    for h in range(GLA_HEADS):
        ks = slice(h * GLA_HEAD_K, (h + 1) * GLA_HEAD_K)
        vs = slice(h * GLA_HEAD_V, (h + 1) * GLA_HEAD_V)
        qh, kh = q[:, ks], k[:, ks]
        vh = v[:, vs].astype(BF16)
        scores = _dot_nt(qh.astype(BF16), kh.astype(BF16)) * masks[nl]
        for li in range(nl):
            eb = e_level[li][:, ks]
            scores = scores + _dot_nt((qh * eb).astype(BF16), (kh * eb).astype(BF16)) * masks[li]
        o_h = _dot(scores.astype(BF16), vh)
        qt = qh * e_cum[:, ks]
        kt = kh * e_last[:, ks]
        vf = v[:, vs]
        inter = []
        for s in range(nseg):
            rows = slice(s * seg_len, (s + 1) * seg_len)
            st = states[s][h]
            inter.append(_dot(qt[rows].astype(BF16), st.astype(BF16)))
            new_states[s][h] = decay_cols[ks, s:s + 1] * st + _dot_tn(
                kt[rows].astype(BF16), vf[rows].astype(BF16))
        o_h = o_h + (inter[0] if nseg == 1 else jnp.concatenate(inter, axis=0))
        outs.append(o_h)
    return jnp.concatenate(outs, axis=1), new_states


def _mixer_kernel(cfg, *refs):
    tb, nseq, carried = cfg["tb"], cfg["nseq"], cfg["carried"]
    refs = list(refs)
    x_ref, mod_ref = refs[:2]
    cst_ref, gst_ref = (None, None) if carried else refs[2:4]
    (npre_ref, npost_ref, win_ref, wal_ref, convw_ref, convb_ref, walpha_ref, balpha_ref,
     gnorm_ref, wpa_ref, wpb_ref, wout_ref, aall_ref, masks_ref, segones_ref,
     xo_ref, cnew_ref, gnew_ref, proj_ref, o_ref, carry_ref, state_ref) = refs[2 if carried else 4:]
    lseq = tb // nseq
    chunk = min(GLA_CHUNK, tb)
    seg_len = min(lseq, chunk)
    step = pl.program_id(1)

    if carried:
        @pl.when(step == 0)
        def _():
            carry_ref[...] = jnp.zeros_like(carry_ref)
            state_ref[...] = jnp.zeros_like(state_ref)

    x = x_ref[...]
    mod = mod_ref[...]
    sh = _rows_from_segments(mod[:, :, 0:D_MODEL], lseq)
    sc = _rows_from_segments(mod[:, :, D_MODEL:2 * D_MODEL], lseq)
    gate = _rows_from_segments(mod[:, :, 2 * D_MODEL:3 * D_MODEL], lseq)
    h = (_rms(x, npre_ref[...]) * (1.0 + sc) + sh).astype(BF16)
    proj_ref[...] = _dot(h, win_ref[...])
    alow = _dot(h, wal_ref[...])

    u = proj_ref[:, OFF_CG:OFF_CG + CONV_DIM] * proj_ref[:, OFF_HIN:OFF_HIN + CONV_DIM]
    prev = carry_ref[...] if carried else cst_ref[...]
    prev0 = _rows_from_segments(prev[:, 0:1, :], lseq)
    prev1 = _rows_from_segments(prev[:, 1:2, :], lseq)
    pos = lax.broadcasted_iota(jnp.int32, (tb, 1), 0) % lseq
    u1 = jnp.where(pos == 0, prev1, pltpu.roll(u, 1, 0))
    u2 = jnp.where(pos == 0, prev0, jnp.where(pos == 1, prev1, pltpu.roll(u, 2, 0)))
    cw = convw_ref[...]
    y_conv = convb_ref[...] + cw[2:3, :] * u + cw[0:1, :] * u2 + cw[1:2, :] * u1
    new_buf = u.reshape(nseq, lseq, CONV_DIM)[:, lseq - 2:lseq, :]
    if carried:
        carry_ref[...] = new_buf
    cnew_ref[...] = new_buf
    branch_a = _dot((proj_ref[:, OFF_BG:OFF_BG + CONV_DIM] * y_conv).astype(BF16), wpa_ref[...])

    z = _dot(alow.astype(BF16), walpha_ref[...]) + balpha_ref[...]
    log_a = (jnp.minimum(z, 0.0) - jnp.log1p(jnp.exp(-jnp.abs(z)))) * (1.0 / GLA_GATE_TAU)
    a_all = aall_ref[...]
    masks = [masks_ref[i] for i in range(masks_ref.shape[0])]
    segones = segones_ref[...]
    nseg = chunk // seg_len
    for c in range(tb // chunk):
        rows = slice(c * chunk, (c + 1) * chunk)
        if carried:
            states = [[state_ref[hh] for hh in range(GLA_HEADS)]]
        else:
            states = [[gst_ref[c * nseg + s, hh] for hh in range(GLA_HEADS)] for s in range(nseg)]
        o_c, new_states = _gla_chunk(
            proj_ref[rows, OFF_Q:OFF_Q + GLA_DK] * (GLA_HEAD_K ** -0.5),
            proj_ref[rows, OFF_K:OFF_K + GLA_DK],
            proj_ref[rows, OFF_V:OFF_V + GLA_DV],
            log_a[rows], a_all, masks, segones, states, seg_len)
        o_ref[rows, :] = o_c
        for s in range(nseg):
            for hh in range(GLA_HEADS):
                if carried:
                    state_ref[hh] = new_states[s][hh]
                else:
                    gnew_ref[c * nseg + s, hh] = new_states[s][hh]
    if carried:
        gnew_ref[0] = state_ref[...]

    gn = gnorm_ref[...]
    normed = []
    for hh in range(GLA_HEADS):
        vs = slice(hh * GLA_HEAD_V, (hh + 1) * GLA_HEAD_V)
        normed.append(_rms(o_ref[:, vs], gn[:, vs]))
    r = proj_ref[:, OFF_R:OFF_R + GLA_DV]
    ob = jnp.concatenate(normed, axis=1) * (r * _sigmoid(r))
    branch_b = _dot(ob.astype(BF16), wpb_ref[...])

    merged = (_sigmoid(proj_ref[:, OFF_GA:OFF_GA + D_MODEL]) * branch_a
              + _sigmoid(proj_ref[:, OFF_GB:OFF_GB + D_MODEL]) * branch_b)
    m = _dot(merged.astype(BF16), wout_ref[...])
    xo_ref[...] = x + gate * _rms(m, npost_ref[...])


def _mixer_call(x2d, mod3, conv_state, gla_state, weights, *, nseq_total, seq_len, carried):
    if carried:
        tb, nseq = MIX_TB_PROMPT, 1
        grid = (nseq_total, seq_len // tb)
        x_map = lambda b, c: (b * (seq_len // tb) + c, 0)
    else:
        nseq = MIX_SEQ_SAMPLE
        tb = nseq * seq_len
        grid = (nseq_total // nseq, 1)
        x_map = lambda b, c: (b, 0)
    chunk = min(GLA_CHUNK, tb)
    seg_len = min(tb // nseq, chunk)
    a_all, masks, segones = _gla_consts(chunk, seg_len)
    cfg = dict(tb=tb, nseq=nseq, carried=carried)
    const2 = lambda b, c: (0, 0)
    const3 = lambda b, c: (0, 0, 0)
    (npre, npost, w_in_main, w_al, conv_w, conv_b, w_alpha, b_alpha, gnorm, wpa, wpb, wout) = weights
    n_tok = x2d.shape[0]
    state_spec = pl.BlockSpec(
        (nseq, GLA_HEADS, GLA_HEAD_K, GLA_HEAD_V), lambda b, c: (b, 0, 0, 0),
        **({} if carried else dict(pipeline_mode=pl.Buffered(1))))
    conv_spec = pl.BlockSpec((nseq, CONV_W - 1, CONV_DIM), lambda b, c: (b, 0, 0))
    state_operands = () if carried else (conv_state, gla_state)
    in_specs = [
        pl.BlockSpec((tb, D_MODEL), x_map),
        pl.BlockSpec((nseq, 1, 3 * D_MODEL), lambda b, c: (b, 0, 0)),
        *(() if carried else (conv_spec, state_spec)),
        _resident((1, D_MODEL), const2),
        _resident((1, D_MODEL), const2),
        _resident((D_MODEL, W_IN_MAIN), const2),
        _resident((D_MODEL, LANES), const2),
        _resident((CONV_W, CONV_DIM), const2),
        _resident((1, CONV_DIM), const2),
        _resident((LANES, GLA_DK), const2),
        _resident((1, GLA_DK), const2),
        _resident((1, GLA_DV), const2),
        _resident((CONV_DIM, D_MODEL), const2),
        _resident((GLA_DV, D_MODEL), const2),
        _resident((D_MODEL, D_MODEL), const2),
        _resident(a_all.shape, const2),
        _resident(masks.shape, const3),
        _resident(segones.shape, const2),
    ]
    out_specs = [pl.BlockSpec((tb, D_MODEL), x_map), conv_spec, state_spec]
    out_shape = [
        jax.ShapeDtypeStruct((n_tok, D_MODEL), F32),
        jax.ShapeDtypeStruct((nseq_total, CONV_W - 1, CONV_DIM), F32),
        jax.ShapeDtypeStruct((nseq_total, GLA_HEADS, GLA_HEAD_K, GLA_HEAD_V), F32),
    ]
    scratch = [
        pltpu.VMEM((tb, W_IN_MAIN), F32),
        pltpu.VMEM((tb, GLA_DV), F32),
        pltpu.VMEM((1, CONV_W - 1, CONV_DIM), F32),
        pltpu.VMEM((GLA_HEADS, GLA_HEAD_K, GLA_HEAD_V), F32),
    ]
    return pl.pallas_call(
        functools.partial(_mixer_kernel, cfg),
        grid=grid,
        in_specs=in_specs,
        out_specs=out_specs,
        out_shape=out_shape,
        scratch_shapes=scratch,
        compiler_params=pltpu.CompilerParams(
            dimension_semantics=("arbitrary", "arbitrary"), vmem_limit_bytes=VMEM_LIMIT_BYTES),
        name="mixer_prompt" if carried else "mixer_sample",
    )(x2d, mod3, *state_operands, npre, npost, w_in_main, w_al, conv_w, conv_b, w_alpha,
      b_alpha, gnorm, wpa, wpb, wout, a_all, masks, segones)


def _wc_kernel(sk_ref, wq_ref, o_ref):
    o_ref[0] = lax.dot_general(sk_ref[0], wq_ref[...], (((1,), (1,)), ((), ())),
                               precision=lax.Precision.HIGHEST,
                               preferred_element_type=F32).astype(BF16)


def _wc_call(subkeys, wq):
    nhp = PEER_HEADS * 2
    out = pl.pallas_call(
        _wc_kernel,
        grid=(nhp,),
        in_specs=[
            pl.BlockSpec((1, PEER_NKEYS, PEER_HALF), lambda i: (i, 0, 0)),
            pl.BlockSpec((D_MODEL, PEER_HALF), lambda i: (0, i)),
        ],
        out_specs=pl.BlockSpec((1, PEER_NKEYS, D_MODEL), lambda i: (i, 0, 0)),
        out_shape=jax.ShapeDtypeStruct((nhp, PEER_NKEYS, D_MODEL), BF16),
        compiler_params=pltpu.CompilerParams(
            dimension_semantics=("arbitrary",), vmem_limit_bytes=VMEM_LIMIT_BYTES),
        name="peer_wc",
    )(subkeys.reshape(nhp, PEER_NKEYS, PEER_HALF), wq)
    return out.reshape(PEER_HEADS, 2, PEER_NKEYS, D_MODEL).transpose(1, 0, 2, 3).reshape(
        2 * PEER_HEADS * PEER_NKEYS, D_MODEL)


def _cand_pairs():
    n = PEER_TOPK + 1
    return [(a, b) for a in range(n) for b in range(n) if (a + 1) * (b + 1) <= n]


def _tree_reduce(op, xs):
    xs = list(xs)
    while len(xs) > 1:
        xs = [op(xs[i], xs[i + 1]) if i + 1 < len(xs) else xs[i] for i in range(0, len(xs), 2)]
    return xs[0]


def _desc_distinct(vals, count):
    out = [_tree_reduce(jnp.maximum, vals)]
    for _ in range(count - 1):
        m = out[-1]
        out.append(_tree_reduce(jnp.maximum, [jnp.where(v < m, v, -jnp.inf) for v in vals]))
    return out


def _store_gate_factors(st_ref, e1_ref, e2_ref, lg, h, max1, max2, half_inv_z_row):
    s1 = st_ref[lg, h * PEER_NKEYS:(h + 1) * PEER_NKEYS, :]
    row2 = (PEER_HEADS + h) * PEER_NKEYS
    s2 = st_ref[lg, row2:row2 + PEER_NKEYS, :]
    e1_ref[lg, h] = jnp.exp(s1 - max1) * half_inv_z_row
    e2_ref[lg, h] = jnp.exp(s2 - max2)


def _peer_select(st_ref, c1_ref, e1_ref, e2_ref, lg):
    nk = PEER_TOPK + 1
    tops = [[None] * PEER_HEADS for _ in range(2)]
    bad = jnp.zeros((1, LANES), F32)
    for p in range(2):
        for h in range(PEER_HEADS):
            r0 = (p * PEER_HEADS + h) * PEER_NKEYS
            s = st_ref[lg, r0:r0 + PEER_NKEYS, :]
            out = [jnp.max(s, axis=0, keepdims=True)]
            for _ in range(nk - 1):
                out.append(jnp.max(jnp.where(s < out[-1], s, -jnp.inf), axis=0, keepdims=True))
            tops[p][h] = out
            n_top = jnp.sum(jnp.where(s >= out[-1], 1.0, 0.0), axis=0, keepdims=True)
            bad = jnp.maximum(bad, jnp.where(n_top != float(nk), 1.0, 0.0))
    packed = [[jnp.concatenate([tops[p][h][k] for h in range(PEER_HEADS)], axis=0)
               for k in range(nk)] for p in range(2)]
    cands = [packed[0][a] + packed[1][b] for (a, b) in _cand_pairs()]
    t = _desc_distinct(cands, nk)
    n_top = _tree_reduce(jnp.add, [jnp.where(c >= t[-1], 1.0, 0.0) for c in cands])
    bad = jnp.maximum(bad, jnp.max(jnp.where(n_top != float(nk), 1.0, 0.0), axis=0, keepdims=True))
    tau = 0.5 * (t[PEER_TOPK - 1] + t[PEER_TOPK])
    z = jnp.ones_like(t[0])
    for k in range(1, PEER_TOPK):
        z = z + jnp.exp(t[k] - t[0])
    half_inv_z = 0.5 / z
    for h in range(PEER_HEADS):
        s1 = st_ref[lg, h * PEER_NKEYS:(h + 1) * PEER_NKEYS, :]
        c1_ref[lg, h] = tau[h:h + 1, :] - s1
        _store_gate_factors(st_ref, e1_ref, e2_ref, lg, h, tops[0][h][0], tops[1][h][0],
                            half_inv_z[h:h + 1, :])

    @pl.when(jnp.max(bad) > 0.0)
    def _():
        for h in range(PEER_HEADS):
            row2 = (PEER_HEADS + h) * PEER_NKEYS
            s2 = st_ref[lg, row2:row2 + PEER_NKEYS, :]
            c1 = c1_ref[lg, h]
            n_sel = jnp.zeros((PEER_NKEYS, LANES), F32)
            for k in range(nk):
                v = tops[1][h][k]
                mult = jnp.sum(jnp.where(s2 == v, 1.0, 0.0), axis=0, keepdims=True)
                n_sel = n_sel + jnp.where(v >= c1, mult, 0.0)
            total = jnp.sum(n_sel, axis=0, keepdims=True)

            @pl.when(jnp.max(jnp.where(total != float(PEER_TOPK), 1.0, 0.0)) > 0.0)
            def _(h=h):
                _peer_select_exact(st_ref, c1_ref, e1_ref, e2_ref, lg, h)


def _peer_select_exact(st_ref, c1_ref, e1_ref, e2_ref, lg, h):
    n_pairs = PEER_TOPK * PEER_TOPK
    row1 = h * PEER_NKEYS
    row2 = (PEER_HEADS + h) * PEER_NKEYS
    s1 = st_ref[lg, row1:row1 + PEER_NKEYS, :]
    s2 = st_ref[lg, row2:row2 + PEER_NKEYS, :]
    key_id = lax.broadcasted_iota(jnp.int32, (1, PEER_NKEYS, LANES), 1).astype(F32)
    slot_id = lax.broadcasted_iota(jnp.int32, (1, PEER_TOPK, 1), 1)

    def pick(k, state):
        rem, rank, vals = state
        m = jnp.max(rem, axis=1, keepdims=True)
        first = jnp.min(jnp.where(rem == m, key_id, float(PEER_NKEYS)), axis=1, keepdims=True)
        hit = key_id == first
        return (jnp.where(hit, -jnp.inf, rem),
                jnp.where(hit, lax.convert_element_type(k, F32), rank),
                jnp.where(slot_id == k, m, vals))
    _, rank, vals = lax.fori_loop(
        0, PEER_TOPK, pick,
        (jnp.stack([s1, s2]), jnp.full((2, PEER_NKEYS, LANES), float(PEER_TOPK), F32),
         jnp.zeros((2, PEER_TOPK, LANES), F32)))

    cand = (vals[0][:, None, :] + vals[1][None, :, :]).reshape(n_pairs, LANES)
    flat = lax.broadcasted_iota(jnp.int32, (n_pairs, 1), 0).astype(F32)

    def pick_pair(k, state):
        rem, taken, z, t0 = state
        m = jnp.max(rem, axis=0, keepdims=True)
        first = jnp.min(jnp.where(rem == m, flat, float(n_pairs)), axis=0, keepdims=True)
        hit = flat == first
        taken = taken + jnp.sum(jnp.where(hit, 1.0, 0.0).reshape(PEER_TOPK, PEER_TOPK, LANES), axis=1)
        t0 = jnp.where(k == 0, m, t0)
        return jnp.where(hit, -jnp.inf, rem), taken, z + jnp.exp(m - t0), t0
    zeros = jnp.zeros((1, LANES), F32)
    _, taken, z, _ = lax.fori_loop(0, PEER_TOPK, pick_pair,
                                   (cand, jnp.zeros((PEER_TOPK, LANES), F32), zeros, zeros))
    n1 = jnp.zeros((PEER_NKEYS, LANES), F32)
    for a in range(PEER_TOPK):
        n1 = n1 + jnp.where(rank[0] == float(a), taken[a:a + 1, :], 0.0)
    e1_ref[lg, h] = jnp.exp(s1 - vals[0][0:1, :]) * (0.5 / z)
    e2_ref[lg, h] = jnp.exp(s2 - vals[1][0:1, :])
    c1_ref[lg, h] = 0.5 - n1
    st_ref[lg, row2:row2 + PEER_NKEYS, :] = -rank[1]


def _peer_kernel(cfg, x_ref, mod_ref, npre_ref, npost_ref, wct_ref, u_ref, vt_ref, y_ref,
                 h2_ref, st_ref, c1_ref, e1_ref, e2_ref, a_ref, w_ref, acc_ref):
    tb, nseq = cfg["tb"], cfg["nseq"]
    lseq = tb // nseq
    e = pl.program_id(1)
    n_e = pl.num_programs(1)
    n_lg = tb // LANES

    def lane_group_rows(seg_vals, lg):
        if nseq == 1:
            return _rows_from_segments(seg_vals, LANES)
        per_lg = nseq // n_lg
        return _rows_from_segments(seg_vals[lg * per_lg:(lg + 1) * per_lg], lseq)

    @pl.when(e == 0)
    def _():
        mod = mod_ref[...]
        for lg in range(n_lg):
            sh = lane_group_rows(mod[:, :, 0:D_MODEL], lg)
            sc = lane_group_rows(mod[:, :, D_MODEL:2 * D_MODEL], lg)
            h2 = _rms(x_ref[lg * LANES:(lg + 1) * LANES, :], npre_ref[...]) * (1.0 + sc) + sh
            h2_ref[:, lg * LANES:(lg + 1) * LANES] = h2.T.astype(BF16)
        score_rows = 2 * PEER_NKEYS
        for r in range(0, 2 * PEER_HEADS * PEER_NKEYS, score_rows):
            st = _dot(wct_ref[r:r + score_rows, :], h2_ref[...])
            for lg in range(n_lg):
                st_ref[lg, r:r + score_rows, :] = st[:, lg * LANES:(lg + 1) * LANES]

        def select(lg, carry):
            _peer_select(st_ref, c1_ref, e1_ref, e2_ref, lg)
            return carry
        lax.fori_loop(0, n_lg, select, 0)
        acc_ref[...] = jnp.zeros_like(acc_ref)

    a_all = _dot(u_ref[...], h2_ref[...])
    for lg in range(n_lg):
        a_ref[lg] = a_all[:, lg * LANES:(lg + 1) * LANES]
    i1_tile = pl.ds(pl.multiple_of(e * PEER_I1, SUBLANES), PEER_I1)

    def gate_lane_group(lg, carry):
        lanes = pl.ds(pl.multiple_of(lg * LANES, LANES), LANES)
        c1t = [c1_ref[lg, h, i1_tile, :] for h in range(PEER_HEADS)]
        e1t = [e1_ref[lg, h, i1_tile, :] for h in range(PEER_HEADS)]
        for i1 in range(PEER_I1):
            g = jnp.zeros((PEER_NKEYS, LANES), F32)
            for h in range(PEER_HEADS):
                row2 = (PEER_HEADS + h) * PEER_NKEYS
                s2 = st_ref[lg, row2:row2 + PEER_NKEYS, :]
                p = e2_ref[lg, h] * e1t[h][i1:i1 + 1, :]
                g = g + jnp.where(s2 >= c1t[h][i1:i1 + 1, :], p, 0.0)
            rows = slice(i1 * PEER_NKEYS, (i1 + 1) * PEER_NKEYS)
            a = a_ref[lg, rows, :]
            act2 = a + a * jnp.tanh(a * (GELU_C0 + (GELU_C0 * GELU_C1) * (a * a)))
            w_ref[rows, lanes] = (act2 * g).astype(BF16)
        return carry
    lax.fori_loop(0, n_lg, gate_lane_group, 0)
    acc_ref[...] += _dot(vt_ref[...], w_ref[...])

    @pl.when(e == n_e - 1)
    def _():
        mod = mod_ref[...]
        for lg in range(n_lg):
            rows = slice(lg * LANES, (lg + 1) * LANES)
            gate = lane_group_rows(mod[:, :, 2 * D_MODEL:3 * D_MODEL], lg)
            out = acc_ref[:, rows].T
            y_ref[rows, :] = x_ref[rows, :] + gate * _rms(out, npost_ref[...])


def _peer_call(x2d, mod3, npre, npost, wct, u_bf, vt_bf, *, nseq_total, seq_len):
    n_tok = x2d.shape[0]
    tb = PEER_TB
    if seq_len >= tb:
        nseq = 1
        mod_map = lambda i, e: (i // (seq_len // tb), 0, 1)
    else:
        nseq = tb // seq_len
        mod_map = lambda i, e: (i, 0, 1)
    n_exp = u_bf.shape[0]
    assert n_exp % PEER_EC == 0 and PEER_I1 % SUBLANES == 0
    cfg = dict(tb=tb, nseq=nseq)
    const2 = lambda i, e: (0, 0)
    return pl.pallas_call(
        functools.partial(_peer_kernel, cfg),
        grid=(n_tok // tb, n_exp // PEER_EC),
        in_specs=[
            pl.BlockSpec((tb, D_MODEL), lambda i, e: (i, 0), pipeline_mode=pl.Buffered(1)),
            pl.BlockSpec((nseq, 1, 3 * D_MODEL), mod_map, pipeline_mode=pl.Buffered(1)),
            _resident((1, D_MODEL), const2),
            _resident((1, D_MODEL), const2),
            _resident(wct.shape, const2),
            pl.BlockSpec((PEER_EC, D_MODEL), lambda i, e: (e, 0)),
            pl.BlockSpec((D_MODEL, PEER_EC), lambda i, e: (0, e)),
        ],
        out_specs=pl.BlockSpec((tb, D_MODEL), lambda i, e: (i, 0)),
        out_shape=jax.ShapeDtypeStruct((n_tok, D_MODEL), F32),
        scratch_shapes=[
            pltpu.VMEM((D_MODEL, tb), BF16),
            pltpu.VMEM((tb // LANES, 2 * PEER_HEADS * PEER_NKEYS, LANES), F32),
            pltpu.VMEM((tb // LANES, PEER_HEADS, PEER_NKEYS, LANES), F32),
            pltpu.VMEM((tb // LANES, PEER_HEADS, PEER_NKEYS, LANES), F32),
            pltpu.VMEM((tb // LANES, PEER_HEADS, PEER_NKEYS, LANES), F32),
            pltpu.VMEM((tb // LANES, PEER_EC, LANES), F32),
            pltpu.VMEM((PEER_EC, tb), BF16),
            pltpu.VMEM((D_MODEL, tb), F32),
        ],
        compiler_params=pltpu.CompilerParams(
            dimension_semantics=("arbitrary", "arbitrary"), vmem_limit_bytes=VMEM_LIMIT_BYTES),
        name="peer_prompt" if nseq == 1 else "peer_sample",
    )(x2d, mod3, npre, npost, wct, u_bf, vt_bf)


def kernel(x_prompt, x_sample, c_prompt, c_sample, state_conv, state_gla, norm_mix_pre, norm_mix_post, norm_ffn_pre, norm_ffn_post, w_ada, b_ada, w_in, conv_w, conv_b, w_alpha, b_alpha, gla_norm_w, w_proj_a, w_proj_b, w_out, peer_wq, peer_subkeys, peer_u, peer_v):
    depth = w_in.shape[0]
    assert depth == 1, "single-layer trunk"
    n_p, t_p, _ = x_prompt.shape
    n_s, t_s, _ = x_sample.shape
    l = 0

    c_all = jnp.concatenate([c_prompt, c_sample], axis=0)
    mod = _ada_call(c_all, w_ada[l], b_ada[l])
    mod_p = mod[:n_p].reshape(n_p, 1, N_MOD * D_MODEL)
    mod_s = mod[n_p:].reshape(n_s, 1, N_MOD * D_MODEL)

    w_in_main = w_in[l][:, :W_IN_MAIN].astype(BF16)
    w_al = jnp.pad(w_in[l][:, W_IN_MAIN:], ((0, 0), (0, LANES - GLA_GATE_RANK))).astype(BF16)
    w_alpha_p = jnp.pad(w_alpha[l], ((0, LANES - GLA_GATE_RANK), (0, 0))).astype(BF16)
    row = lambda a: a.reshape(1, -1)
    mixer_w = (row(norm_mix_pre[l]), row(norm_mix_post[l]), w_in_main, w_al, conv_w[l], row(conv_b[l]),
               w_alpha_p, row(b_alpha[l]), row(gla_norm_w[l]), w_proj_a[l].astype(BF16),
               w_proj_b[l].astype(BF16), w_out[l].astype(BF16))

    x1_p, conv_p, gla_p = _mixer_call(
        x_prompt.reshape(n_p * t_p, D_MODEL), mod_p, None, None, mixer_w,
        nseq_total=n_p, seq_len=t_p, carried=True)
    x1_s, conv_s, gla_s = _mixer_call(
        x_sample.reshape(n_s * t_s, D_MODEL), mod_s, state_conv[l], state_gla[l], mixer_w,
        nseq_total=n_s, seq_len=t_s, carried=False)

    wct = _wc_call(peer_subkeys[l], peer_wq[l])
    u_bf = peer_u[l].astype(BF16)
    vt_bf = peer_v[l].astype(BF16).T
    npre2, npost2 = row(norm_ffn_pre[l]), row(norm_ffn_post[l])
    y_p = _peer_call(x1_p, mod_p, npre2, npost2, wct, u_bf, vt_bf, nseq_total=n_p, seq_len=t_p)
    y_s = _peer_call(x1_s, mod_s, npre2, npost2, wct, u_bf, vt_bf, nseq_total=n_s, seq_len=t_s)

    return (y_p.reshape(n_p, t_p, D_MODEL), y_s.reshape(n_s, t_s, D_MODEL),
            conv_p[None], gla_p[None], conv_s[None], gla_s[None])
```

```python
import functools
import math

import jax
import jax.numpy as jnp
import numpy as np
from jax import lax
from jax.experimental import pallas as pl
from jax.experimental.pallas import tpu as pltpu

F32 = jnp.float32
BF16 = jnp.bfloat16

D_MODEL = 1024
CONV_DIM = 512
CONV_W = 3
GLA_HEADS = 4
GLA_DK = 512
GLA_DV = 1024
GLA_HEAD_K = 128
GLA_HEAD_V = 256
GLA_GATE_RANK = 16
GLA_GATE_TAU = 16.0
PEER_HEADS = 8
PEER_NKEYS = 128
PEER_HALF = 128
PEER_TOPK = 16
N_MOD = 6
EPS = 1e-6
W_IN_MAIN = 3 * CONV_DIM + 2 * GLA_DK + 2 * GLA_DV + 2 * D_MODEL
OFF_BG, OFF_CG, OFF_HIN = 0, 512, 1024
OFF_Q, OFF_K, OFF_V, OFF_R, OFF_GA, OFF_GB = 1536, 2048, 2560, 3584, 4608, 5632

LANES = 128
SUBLANES = 8
VMEM_LIMIT_BYTES = 58 * 1024 * 1024

GLA_CHUNK = 128
MIX_TB_PROMPT = 512
MIX_SEQ_SAMPLE = 16
PEER_TB = 512
PEER_EC = 2048
PEER_I1 = PEER_EC // PEER_NKEYS
GELU_C0 = math.sqrt(2.0 / math.pi)
GELU_C1 = 0.044715


def _dot(a, b):
    return jnp.dot(a, b, preferred_element_type=F32)


def _dot_nt(a, b):
    return lax.dot_general(a, b, (((1,), (1,)), ((), ())), preferred_element_type=F32)


def _dot_tn(a, b):
    return lax.dot_general(a, b, (((0,), (0,)), ((), ())), preferred_element_type=F32)


def _split2(x):
    p0 = x.astype(BF16)
    return p0, (x - p0.astype(F32)).astype(BF16)


def _rms(x, w):
    return x * lax.rsqrt(jnp.mean(x * x, axis=-1, keepdims=True) + EPS) * w


def _sigmoid(x):
    return 1.0 / (1.0 + jnp.exp(-x))


def _resident(shape, index_map):
    return pl.BlockSpec(shape, index_map, pipeline_mode=pl.Buffered(1))


def _rows_from_segments(seg_vals, seg_len):
    nseg, _, d = seg_vals.shape
    return jnp.broadcast_to(seg_vals, (nseg, seg_len, d)).reshape(nseg * seg_len, d)


def _ada_kernel(c_ref, w_ref, b_ref, o_ref):
    c = c_ref[...]
    a = (c * _sigmoid(c)).astype(BF16)
    o_ref[...] = _dot(a, w_ref[...].astype(BF16)) + b_ref[...]


def _ada_call(c_all, w_ada, b_ada):
    rows = c_all.shape[0]
    ncols = w_ada.shape[1]
    bn = 1536
    return pl.pallas_call(
        _ada_kernel,
        grid=(ncols // bn,),
        in_specs=[
            pl.BlockSpec((rows, D_MODEL), lambda j: (0, 0)),
            pl.BlockSpec((D_MODEL, bn), lambda j: (0, j)),
            pl.BlockSpec((1, bn), lambda j: (0, j)),
        ],
        out_specs=pl.BlockSpec((rows, bn), lambda j: (0, j)),
        out_shape=jax.ShapeDtypeStruct((rows, ncols), F32),
        compiler_params=pltpu.CompilerParams(
            dimension_semantics=("arbitrary",), vmem_limit_bytes=VMEM_LIMIT_BYTES),
        name="ada_mod",
    )(c_all, w_ada, b_ada.reshape(1, ncols))


def _gla_levels(seg_len):
    return [b for b in (1, 2, 4, 8, 16, 32, 64) if 2 * b <= seg_len]


def _gla_consts(chunk, seg_len):
    t = np.arange(chunk)
    seg = t // seg_len
    tri = (t[None, :] <= t[:, None]) & (seg[None, :] == seg[:, None])
    masks = []
    for b in _gla_levels(seg_len):
        parent = t // (2 * b)
        upper = (t % (2 * b)) >= b
        masks.append((parent[:, None] == parent[None, :]) & upper[:, None] & ~upper[None, :])
    masks.append(np.eye(chunk, dtype=bool))
    segones = np.zeros((chunk, LANES), np.float32)
    segones[t, seg] = 1.0
    return (jnp.asarray(tri.astype(np.float32), BF16),
            jnp.asarray(np.stack(masks).astype(np.float32)),
            jnp.asarray(segones, BF16))


def _group_row(x, group, row):
    n, d = x.shape
    if group >= SUBLANES:
        picked = x.reshape(n // group, group, d)[:, row:row + 1, :]
        return jnp.broadcast_to(picked, (n // group, group, d)).reshape(n, d)
    pos = lax.broadcasted_iota(jnp.int32, (n, 1), 0) % group
    out = x
    for p in range(group):
        if p != row:
            out = jnp.where(pos == p, pltpu.roll(x, (p - row) % n, 0), out)
    return out


def _gla_chunk(q, k, v, la, a_all, masks, segones, states, seg_len):
    c = q.shape[0]
    nseg = c // seg_len
    levels = _gla_levels(seg_len)
    nl = len(levels)
    pieces = _split2(la)
    cum = sum(_dot(a_all, p) for p in pieces)
    cum_last_t = sum(_dot_tn(p, segones) for p in pieces)
    decay_cols = jnp.exp(cum_last_t)
    e_cum = jnp.exp(cum)
    e_last = jnp.exp(_group_row(cum, seg_len, seg_len - 1) - cum)
    e_level = [jnp.exp(-jnp.abs(cum - _group_row(cum, 2 * b, b - 1))) for b in levels]
    outs = []
    new_states = [[None] * GLA_HEADS for _ in range(nseg)]
    for h in range(GLA_HEADS):
        ks = slice(h * GLA_HEAD_K, (h + 1) * GLA_HEAD_K)
        vs = slice(h * GLA_HEAD_V, (h + 1) * GLA_HEAD_V)
        qh, kh = q[:, ks], k[:, ks]
        vh = v[:, vs].astype(BF16)
        scores = _dot_nt(qh.astype(BF16), kh.astype(BF16)) * masks[nl]
        for li in range(nl):
            eb = e_level[li][:, ks]
            scores = scores + _dot_nt((qh * eb).astype(BF16), (kh * eb).astype(BF16)) * masks[li]
        o_h = _dot(scores.astype(BF16), vh)
        qt = qh * e_cum[:, ks]
        kt = kh * e_last[:, ks]
        vf = v[:, vs]
        inter = []
        for s in range(nseg):
            rows = slice(s * seg_len, (s + 1) * seg_len)
            st = states[s][h]
            inter.append(_dot(qt[rows].astype(BF16), st.astype(BF16)))
            new_states[s][h] = decay_cols[ks, s:s + 1] * st + _dot_tn(
                kt[rows].astype(BF16), vf[rows].astype(BF16))
        o_h = o_h + (inter[0] if nseg == 1 else jnp.concatenate(inter, axis=0))
        outs.append(o_h)
    return jnp.concatenate(outs, axis=1), new_states


def _mixer_kernel(cfg, *refs):
    tb, nseq, carried = cfg["tb"], cfg["nseq"], cfg["carried"]
    refs = list(refs)
    x_ref, mod_ref = refs[:2]
    cst_ref, gst_ref = (None, None) if carried else refs[2:4]
    (npre_ref, npost_ref, win_ref, wal_ref, convw_ref, convb_ref, walpha_ref, balpha_ref,
     gnorm_ref, wpa_ref, wpb_ref, wout_ref, aall_ref, masks_ref, segones_ref,
     xo_ref, cnew_ref, gnew_ref, proj_ref, o_ref, carry_ref, state_ref) = refs[2 if carried else 4:]
    lseq = tb // nseq
    chunk = min(GLA_CHUNK, tb)
    seg_len = min(lseq, chunk)
    step = pl.program_id(1)

    if carried:
        @pl.when(step == 0)
        def _():
            carry_ref[...] = jnp.zeros_like(carry_ref)
            state_ref[...] = jnp.zeros_like(state_ref)

    x = x_ref[...]
    mod = mod_ref[...]
    sh = _rows_from_segments(mod[:, :, 0:D_MODEL], lseq)
    sc = _rows_from_segments(mod[:, :, D_MODEL:2 * D_MODEL], lseq)
    gate = _rows_from_segments(mod[:, :, 2 * D_MODEL:3 * D_MODEL], lseq)
    h = (_rms(x, npre_ref[...]) * (1.0 + sc) + sh).astype(BF16)
    proj_ref[...] = _dot(h, win_ref[...])
    alow = _dot(h, wal_ref[...])

    u = proj_ref[:, OFF_CG:OFF_CG + CONV_DIM] * proj_ref[:, OFF_HIN:OFF_HIN + CONV_DIM]
    prev = carry_ref[...] if carried else cst_ref[...]
    prev0 = _rows_from_segments(prev[:, 0:1, :], lseq)
    prev1 = _rows_from_segments(prev[:, 1:2, :], lseq)
    pos = lax.broadcasted_iota(jnp.int32, (tb, 1), 0) % lseq
    u1 = jnp.where(pos == 0, prev1, pltpu.roll(u, 1, 0))
    u2 = jnp.where(pos == 0, prev0, jnp.where(pos == 1, prev1, pltpu.roll(u, 2, 0)))
    cw = convw_ref[...]
    y_conv = convb_ref[...] + cw[2:3, :] * u + cw[0:1, :] * u2 + cw[1:2, :] * u1
    new_buf = u.reshape(nseq, lseq, CONV_DIM)[:, lseq - 2:lseq, :]
    if carried:
        carry_ref[...] = new_buf
    cnew_ref[...] = new_buf
    branch_a = _dot((proj_ref[:, OFF_BG:OFF_BG + CONV_DIM] * y_conv).astype(BF16), wpa_ref[...])

    z = _dot(alow.astype(BF16), walpha_ref[...]) + balpha_ref[...]
    log_a = (jnp.minimum(z, 0.0) - jnp.log1p(jnp.exp(-jnp.abs(z)))) * (1.0 / GLA_GATE_TAU)
    a_all = aall_ref[...]
    masks = [masks_ref[i] for i in range(masks_ref.shape[0])]
    segones = segones_ref[...]
    nseg = chunk // seg_len
    for c in range(tb // chunk):
        rows = slice(c * chunk, (c + 1) * chunk)
        if carried:
            states = [[state_ref[hh] for hh in range(GLA_HEADS)]]
        else:
            states = [[gst_ref[c * nseg + s, hh] for hh in range(GLA_HEADS)] for s in range(nseg)]
        o_c, new_states = _gla_chunk(
            proj_ref[rows, OFF_Q:OFF_Q + GLA_DK] * (GLA_HEAD_K ** -0.5),
            proj_ref[rows, OFF_K:OFF_K + GLA_DK],
            proj_ref[rows, OFF_V:OFF_V + GLA_DV],
            log_a[rows], a_all, masks, segones, states, seg_len)
        o_ref[rows, :] = o_c
        for s in range(nseg):
            for hh in range(GLA_HEADS):
                if carried:
                    state_ref[hh] = new_states[s][hh]
                else:
                    gnew_ref[c * nseg + s, hh] = new_states[s][hh]
    if carried:
        gnew_ref[0] = state_ref[...]

    gn = gnorm_ref[...]
    normed = []
    for hh in range(GLA_HEADS):
        vs = slice(hh * GLA_HEAD_V, (hh + 1) * GLA_HEAD_V)
        normed.append(_rms(o_ref[:, vs], gn[:, vs]))
    r = proj_ref[:, OFF_R:OFF_R + GLA_DV]
    ob = jnp.concatenate(normed, axis=1) * (r * _sigmoid(r))
    branch_b = _dot(ob.astype(BF16), wpb_ref[...])

    merged = (_sigmoid(proj_ref[:, OFF_GA:OFF_GA + D_MODEL]) * branch_a
              + _sigmoid(proj_ref[:, OFF_GB:OFF_GB + D_MODEL]) * branch_b)
    m = _dot(merged.astype(BF16), wout_ref[...])
    xo_ref[...] = x + gate * _rms(m, npost_ref[...])


def _mixer_call(x2d, mod3, conv_state, gla_state, weights, *, nseq_total, seq_len, carried):
    if carried:
        tb, nseq = MIX_TB_PROMPT, 1
        grid = (nseq_total, seq_len // tb)
        x_map = lambda b, c: (b * (seq_len // tb) + c, 0)
    else:
        nseq = MIX_SEQ_SAMPLE
        tb = nseq * seq_len
        grid = (nseq_total // nseq, 1)
        x_map = lambda b, c: (b, 0)
    chunk = min(GLA_CHUNK, tb)
    seg_len = min(tb // nseq, chunk)
    a_all, masks, segones = _gla_consts(chunk, seg_len)
    cfg = dict(tb=tb, nseq=nseq, carried=carried)
    const2 = lambda b, c: (0, 0)
    const3 = lambda b, c: (0, 0, 0)
    (npre, npost, w_in_main, w_al, conv_w, conv_b, w_alpha, b_alpha, gnorm, wpa, wpb, wout) = weights
    n_tok = x2d.shape[0]
    state_spec = pl.BlockSpec(
        (nseq, GLA_HEADS, GLA_HEAD_K, GLA_HEAD_V), lambda b, c: (b, 0, 0, 0),
        **({} if carried else dict(pipeline_mode=pl.Buffered(1))))
    conv_spec = pl.BlockSpec((nseq, CONV_W - 1, CONV_DIM), lambda b, c: (b, 0, 0))
    state_operands = () if carried else (conv_state, gla_state)
    in_specs = [
        pl.BlockSpec((tb, D_MODEL), x_map),
        pl.BlockSpec((nseq, 1, 3 * D_MODEL), lambda b, c: (b, 0, 0)),
        *(() if carried else (conv_spec, state_spec)),
        _resident((1, D_MODEL), const2),
        _resident((1, D_MODEL), const2),
        _resident((D_MODEL, W_IN_MAIN), const2),
        _resident((D_MODEL, LANES), const2),
        _resident((CONV_W, CONV_DIM), const2),
        _resident((1, CONV_DIM), const2),
        _resident((LANES, GLA_DK), const2),
        _resident((1, GLA_DK), const2),
        _resident((1, GLA_DV), const2),
        _resident((CONV_DIM, D_MODEL), const2),
        _resident((GLA_DV, D_MODEL), const2),
        _resident((D_MODEL, D_MODEL), const2),
        _resident(a_all.shape, const2),
        _resident(masks.shape, const3),
        _resident(segones.shape, const2),
    ]
    out_specs = [pl.BlockSpec((tb, D_MODEL), x_map), conv_spec, state_spec]
    out_shape = [
        jax.ShapeDtypeStruct((n_tok, D_MODEL), F32),
        jax.ShapeDtypeStruct((nseq_total, CONV_W - 1, CONV_DIM), F32),
        jax.ShapeDtypeStruct((nseq_total, GLA_HEADS, GLA_HEAD_K, GLA_HEAD_V), F32),
    ]
    scratch = [
        pltpu.VMEM((tb, W_IN_MAIN), F32),
        pltpu.VMEM((tb, GLA_DV), F32),
        pltpu.VMEM((1, CONV_W - 1, CONV_DIM), F32),
        pltpu.VMEM((GLA_HEADS, GLA_HEAD_K, GLA_HEAD_V), F32),
    ]
    return pl.pallas_call(
        functools.partial(_mixer_kernel, cfg),
        grid=grid,
        in_specs=in_specs,
        out_specs=out_specs,
        out_shape=out_shape,
        scratch_shapes=scratch,
        compiler_params=pltpu.CompilerParams(
            dimension_semantics=("arbitrary", "arbitrary"), vmem_limit_bytes=VMEM_LIMIT_BYTES),
        name="mixer_prompt" if carried else "mixer_sample",
    )(x2d, mod3, *state_operands, npre, npost, w_in_main, w_al, conv_w, conv_b, w_alpha,
      b_alpha, gnorm, wpa, wpb, wout, a_all, masks, segones)


def _wc_kernel(sk_ref, wq_ref, o_ref):
    o_ref[0] = lax.dot_general(sk_ref[0], wq_ref[...], (((1,), (1,)), ((), ())),
                               precision=lax.Precision.HIGHEST,
                               preferred_element_type=F32).astype(BF16)


def _wc_call(subkeys, wq):
    nhp = PEER_HEADS * 2
    out = pl.pallas_call(
        _wc_kernel,
        grid=(nhp,),
        in_specs=[
            pl.BlockSpec((1, PEER_NKEYS, PEER_HALF), lambda i: (i, 0, 0)),
            pl.BlockSpec((D_MODEL, PEER_HALF), lambda i: (0, i)),
        ],
        out_specs=pl.BlockSpec((1, PEER_NKEYS, D_MODEL), lambda i: (i, 0, 0)),
        out_shape=jax.ShapeDtypeStruct((nhp, PEER_NKEYS, D_MODEL), BF16),
        compiler_params=pltpu.CompilerParams(
            dimension_semantics=("arbitrary",), vmem_limit_bytes=VMEM_LIMIT_BYTES),
        name="peer_wc",
    )(subkeys.reshape(nhp, PEER_NKEYS, PEER_HALF), wq)
    return out.reshape(PEER_HEADS, 2, PEER_NKEYS, D_MODEL).transpose(1, 0, 2, 3).reshape(
        2 * PEER_HEADS * PEER_NKEYS, D_MODEL)


def _cand_pairs():
    n = PEER_TOPK + 1
    return [(a, b) for a in range(n) for b in range(n) if (a + 1) * (b + 1) <= n]


def _tree_reduce(op, xs):
    xs = list(xs)
    while len(xs) > 1:
        xs = [op(xs[i], xs[i + 1]) if i + 1 < len(xs) else xs[i] for i in range(0, len(xs), 2)]
    return xs[0]


def _desc_distinct(vals, count):
    out = [_tree_reduce(jnp.maximum, vals)]
    for _ in range(count - 1):
        m = out[-1]
        out.append(_tree_reduce(jnp.maximum, [jnp.where(v < m, v, -jnp.inf) for v in vals]))
    return out


def _store_gate_factors(st_ref, e1_ref, e2_ref, lg, h, max1, max2, half_inv_z_row):
    s1 = st_ref[lg, h * PEER_NKEYS:(h + 1) * PEER_NKEYS, :]
    row2 = (PEER_HEADS + h) * PEER_NKEYS
    s2 = st_ref[lg, row2:row2 + PEER_NKEYS, :]
    e1_ref[lg, h] = jnp.exp(s1 - max1) * half_inv_z_row
    e2_ref[lg, h] = jnp.exp(s2 - max2)


def _peer_select(st_ref, c1_ref, e1_ref, e2_ref, lg):
    nk = PEER_TOPK + 1
    tops = [[None] * PEER_HEADS for _ in range(2)]
    bad = jnp.zeros((1, LANES), F32)
    for p in range(2):
        for h in range(PEER_HEADS):
            r0 = (p * PEER_HEADS + h) * PEER_NKEYS
            s = st_ref[lg, r0:r0 + PEER_NKEYS, :]
            out = [jnp.max(s, axis=0, keepdims=True)]
            for _ in range(nk - 1):
                out.append(jnp.max(jnp.where(s < out[-1], s, -jnp.inf), axis=0, keepdims=True))
            tops[p][h] = out
            n_top = jnp.sum(jnp.where(s >= out[-1], 1.0, 0.0), axis=0, keepdims=True)
            bad = jnp.maximum(bad, jnp.where(n_top != float(nk), 1.0, 0.0))
    packed = [[jnp.concatenate([tops[p][h][k] for h in range(PEER_HEADS)], axis=0)
               for k in range(nk)] for p in range(2)]
    cands = [packed[0][a] + packed[1][b] for (a, b) in _cand_pairs()]
    t = _desc_distinct(cands, nk)
    n_top = _tree_reduce(jnp.add, [jnp.where(c >= t[-1], 1.0, 0.0) for c in cands])
    bad = jnp.maximum(bad, jnp.max(jnp.where(n_top != float(nk), 1.0, 0.0), axis=0, keepdims=True))
    tau = 0.5 * (t[PEER_TOPK - 1] + t[PEER_TOPK])
    z = jnp.ones_like(t[0])
    for k in range(1, PEER_TOPK):
        z = z + jnp.exp(t[k] - t[0])
    half_inv_z = 0.5 / z
    for h in range(PEER_HEADS):
        s1 = st_ref[lg, h * PEER_NKEYS:(h + 1) * PEER_NKEYS, :]
        c1_ref[lg, h] = tau[h:h + 1, :] - s1
        _store_gate_factors(st_ref, e1_ref, e2_ref, lg, h, tops[0][h][0], tops[1][h][0],
                            half_inv_z[h:h + 1, :])

    @pl.when(jnp.max(bad) > 0.0)
    def _():
        for h in range(PEER_HEADS):
            row2 = (PEER_HEADS + h) * PEER_NKEYS
            s2 = st_ref[lg, row2:row2 + PEER_NKEYS, :]
            c1 = c1_ref[lg, h]
            n_sel = jnp.zeros((PEER_NKEYS, LANES), F32)
            for k in range(nk):
                v = tops[1][h][k]
                mult = jnp.sum(jnp.where(s2 == v, 1.0, 0.0), axis=0, keepdims=True)
                n_sel = n_sel + jnp.where(v >= c1, mult, 0.0)
            total = jnp.sum(n_sel, axis=0, keepdims=True)

            @pl.when(jnp.max(jnp.where(total != float(PEER_TOPK), 1.0, 0.0)) > 0.0)
            def _(h=h):
                _peer_select_exact(st_ref, c1_ref, e1_ref, e2_ref, lg, h)


def _peer_select_exact(st_ref, c1_ref, e1_ref, e2_ref, lg, h):
    n_pairs = PEER_TOPK * PEER_TOPK
    row1 = h * PEER_NKEYS
    row2 = (PEER_HEADS + h) * PEER_NKEYS
    s1 = st_ref[lg, row1:row1 + PEER_NKEYS, :]
    s2 = st_ref[lg, row2:row2 + PEER_NKEYS, :]
    key_id = lax.broadcasted_iota(jnp.int32, (1, PEER_NKEYS, LANES), 1).astype(F32)
    slot_id = lax.broadcasted_iota(jnp.int32, (1, PEER_TOPK, 1), 1)

    def pick(k, state):
        rem, rank, vals = state
        m = jnp.max(rem, axis=1, keepdims=True)
        first = jnp.min(jnp.where(rem == m, key_id, float(PEER_NKEYS)), axis=1, keepdims=True)
        hit = key_id == first
        return (jnp.where(hit, -jnp.inf, rem),
                jnp.where(hit, lax.convert_element_type(k, F32), rank),
                jnp.where(slot_id == k, m, vals))
    _, rank, vals = lax.fori_loop(
        0, PEER_TOPK, pick,
        (jnp.stack([s1, s2]), jnp.full((2, PEER_NKEYS, LANES), float(PEER_TOPK), F32),
         jnp.zeros((2, PEER_TOPK, LANES), F32)))

    cand = (vals[0][:, None, :] + vals[1][None, :, :]).reshape(n_pairs, LANES)
    flat = lax.broadcasted_iota(jnp.int32, (n_pairs, 1), 0).astype(F32)

    def pick_pair(k, state):
        rem, taken, z, t0 = state
        m = jnp.max(rem, axis=0, keepdims=True)
        first = jnp.min(jnp.where(rem == m, flat, float(n_pairs)), axis=0, keepdims=True)
        hit = flat == first
        taken = taken + jnp.sum(jnp.where(hit, 1.0, 0.0).reshape(PEER_TOPK, PEER_TOPK, LANES), axis=1)
        t0 = jnp.where(k == 0, m, t0)
        return jnp.where(hit, -jnp.inf, rem), taken, z + jnp.exp(m - t0), t0
    zeros = jnp.zeros((1, LANES), F32)
    _, taken, z, _ = lax.fori_loop(0, PEER_TOPK, pick_pair,
                                   (cand, jnp.zeros((PEER_TOPK, LANES), F32), zeros, zeros))
    n1 = jnp.zeros((PEER_NKEYS, LANES), F32)
    for a in range(PEER_TOPK):
        n1 = n1 + jnp.where(rank[0] == float(a), taken[a:a + 1, :], 0.0)
    e1_ref[lg, h] = jnp.exp(s1 - vals[0][0:1, :]) * (0.5 / z)
    e2_ref[lg, h] = jnp.exp(s2 - vals[1][0:1, :])
    c1_ref[lg, h] = 0.5 - n1
    st_ref[lg, row2:row2 + PEER_NKEYS, :] = -rank[1]


def _peer_kernel(cfg, x_ref, mod_ref, npre_ref, npost_ref, wct_ref, u_ref, vt_ref, y_ref,
                 h2_ref, st_ref, c1_ref, e1_ref, e2_ref, a_ref, w_ref, acc_ref):
    tb, nseq = cfg["tb"], cfg["nseq"]
    lseq = tb // nseq
    e = pl.program_id(1)
    n_e = pl.num_programs(1)
    n_lg = tb // LANES

    def lane_group_rows(seg_vals, lg):
        if nseq == 1:
            return _rows_from_segments(seg_vals, LANES)
        per_lg = nseq // n_lg
        return _rows_from_segments(seg_vals[lg * per_lg:(lg + 1) * per_lg], lseq)

    @pl.when(e == 0)
    def _():
        mod = mod_ref[...]
        for lg in range(n_lg):
            sh = lane_group_rows(mod[:, :, 0:D_MODEL], lg)
            sc = lane_group_rows(mod[:, :, D_MODEL:2 * D_MODEL], lg)
            h2 = _rms(x_ref[lg * LANES:(lg + 1) * LANES, :], npre_ref[...]) * (1.0 + sc) + sh
            h2_ref[:, lg * LANES:(lg + 1) * LANES] = h2.T.astype(BF16)
        score_rows = 2 * PEER_NKEYS
        for r in range(0, 2 * PEER_HEADS * PEER_NKEYS, score_rows):
            st = _dot(wct_ref[r:r + score_rows, :], h2_ref[...])
            for lg in range(n_lg):
                st_ref[lg, r:r + score_rows, :] = st[:, lg * LANES:(lg + 1) * LANES]

        def select(lg, carry):
            _peer_select(st_ref, c1_ref, e1_ref, e2_ref, lg)
            return carry
        lax.fori_loop(0, n_lg, select, 0)
        acc_ref[...] = jnp.zeros_like(acc_ref)

    a_all = _dot(u_ref[...], h2_ref[...])
    for lg in range(n_lg):
        a_ref[lg] = a_all[:, lg * LANES:(lg + 1) * LANES]
    i1_tile = pl.ds(pl.multiple_of(e * PEER_I1, SUBLANES), PEER_I1)

    def gate_lane_group(lg, carry):
        lanes = pl.ds(pl.multiple_of(lg * LANES, LANES), LANES)
        c1t = [c1_ref[lg, h, i1_tile, :] for h in range(PEER_HEADS)]
        e1t = [e1_ref[lg, h, i1_tile, :] for h in range(PEER_HEADS)]
        for i1 in range(PEER_I1):
            g = jnp.zeros((PEER_NKEYS, LANES), F32)
            for h in range(PEER_HEADS):
                row2 = (PEER_HEADS + h) * PEER_NKEYS
                s2 = st_ref[lg, row2:row2 + PEER_NKEYS, :]
                p = e2_ref[lg, h] * e1t[h][i1:i1 + 1, :]
                g = g + jnp.where(s2 >= c1t[h][i1:i1 + 1, :], p, 0.0)
            rows = slice(i1 * PEER_NKEYS, (i1 + 1) * PEER_NKEYS)
            a = a_ref[lg, rows, :]
            act2 = a + a * jnp.tanh(a * (GELU_C0 + (GELU_C0 * GELU_C1) * (a * a)))
            w_ref[rows, lanes] = (act2 * g).astype(BF16)
        return carry
    lax.fori_loop(0, n_lg, gate_lane_group, 0)
    acc_ref[...] += _dot(vt_ref[...], w_ref[...])

    @pl.when(e == n_e - 1)
    def _():
        mod = mod_ref[...]
        for lg in range(n_lg):
            rows = slice(lg * LANES, (lg + 1) * LANES)
            gate = lane_group_rows(mod[:, :, 2 * D_MODEL:3 * D_MODEL], lg)
            out = acc_ref[:, rows].T
            y_ref[rows, :] = x_ref[rows, :] + gate * _rms(out, npost_ref[...])


def _peer_call(x2d, mod3, npre, npost, wct, u_bf, vt_bf, *, nseq_total, seq_len):
    n_tok = x2d.shape[0]
    tb = PEER_TB
    if seq_len >= tb:
        nseq = 1
        mod_map = lambda i, e: (i // (seq_len // tb), 0, 1)
    else:
        nseq = tb // seq_len
        mod_map = lambda i, e: (i, 0, 1)
    n_exp = u_bf.shape[0]
    assert n_exp % PEER_EC == 0 and PEER_I1 % SUBLANES == 0
    cfg = dict(tb=tb, nseq=nseq)
    const2 = lambda i, e: (0, 0)
    return pl.pallas_call(
        functools.partial(_peer_kernel, cfg),
        grid=(n_tok // tb, n_exp // PEER_EC),
        in_specs=[
            pl.BlockSpec((tb, D_MODEL), lambda i, e: (i, 0)),
            pl.BlockSpec((nseq, 1, 3 * D_MODEL), mod_map,
                         **({} if nseq == 1 else dict(pipeline_mode=pl.Buffered(1)))),
            _resident((1, D_MODEL), const2),
            _resident((1, D_MODEL), const2),
            _resident(wct.shape, const2),
            pl.BlockSpec((PEER_EC, D_MODEL), lambda i, e: (e, 0)),
            pl.BlockSpec((D_MODEL, PEER_EC), lambda i, e: (0, e)),
        ],
        out_specs=pl.BlockSpec((tb, D_MODEL), lambda i, e: (i, 0)),
        out_shape=jax.ShapeDtypeStruct((n_tok, D_MODEL), F32),
        scratch_shapes=[
            pltpu.VMEM((D_MODEL, tb), BF16),
            pltpu.VMEM((tb // LANES, 2 * PEER_HEADS * PEER_NKEYS, LANES), F32),
            pltpu.VMEM((tb // LANES, PEER_HEADS, PEER_NKEYS, LANES), F32),
            pltpu.VMEM((tb // LANES, PEER_HEADS, PEER_NKEYS, LANES), F32),
            pltpu.VMEM((tb // LANES, PEER_HEADS, PEER_NKEYS, LANES), F32),
            pltpu.VMEM((tb // LANES, PEER_EC, LANES), F32),
            pltpu.VMEM((PEER_EC, tb), BF16),
            pltpu.VMEM((D_MODEL, tb), F32),
        ],
        compiler_params=pltpu.CompilerParams(
            dimension_semantics=("arbitrary", "arbitrary"), vmem_limit_bytes=VMEM_LIMIT_BYTES),
        name="peer_prompt" if nseq == 1 else "peer_sample",
    )(x2d, mod3, npre, npost, wct, u_bf, vt_bf)


def kernel(x_prompt, x_sample, c_prompt, c_sample, state_conv, state_gla, norm_mix_pre, norm_mix_post, norm_ffn_pre, norm_ffn_post, w_ada, b_ada, w_in, conv_w, conv_b, w_alpha, b_alpha, gla_norm_w, w_proj_a, w_proj_b, w_out, peer_wq, peer_subkeys, peer_u, peer_v):
    depth = w_in.shape[0]
    assert depth == 1, "single-layer trunk"
    n_p, t_p, _ = x_prompt.shape
    n_s, t_s, _ = x_sample.shape
    l = 0

    c_all = jnp.concatenate([c_prompt, c_sample], axis=0)
    mod = _ada_call(c_all, w_ada[l], b_ada[l])
    mod_p = mod[:n_p].reshape(n_p, 1, N_MOD * D_MODEL)
    mod_s = mod[n_p:].reshape(n_s, 1, N_MOD * D_MODEL)

    w_in_main = w_in[l][:, :W_IN_MAIN].astype(BF16)
    w_al = jnp.pad(w_in[l][:, W_IN_MAIN:], ((0, 0), (0, LANES - GLA_GATE_RANK))).astype(BF16)
    w_alpha_p = jnp.pad(w_alpha[l], ((0, LANES - GLA_GATE_RANK), (0, 0))).astype(BF16)
    row = lambda a: a.reshape(1, -1)
    mixer_w = (row(norm_mix_pre[l]), row(norm_mix_post[l]), w_in_main, w_al, conv_w[l], row(conv_b[l]),
               w_alpha_p, row(b_alpha[l]), row(gla_norm_w[l]), w_proj_a[l].astype(BF16),
               w_proj_b[l].astype(BF16), w_out[l].astype(BF16))

    x1_p, conv_p, gla_p = _mixer_call(
        x_prompt.reshape(n_p * t_p, D_MODEL), mod_p, None, None, mixer_w,
        nseq_total=n_p, seq_len=t_p, carried=True)
    x1_s, conv_s, gla_s = _mixer_call(
        x_sample.reshape(n_s * t_s, D_MODEL), mod_s, state_conv[l], state_gla[l], mixer_w,
        nseq_total=n_s, seq_len=t_s, carried=False)

    wct = _wc_call(peer_subkeys[l], peer_wq[l])
    u_bf = peer_u[l].astype(BF16)
    vt_bf = peer_v[l].astype(BF16).T
    npre2, npost2 = row(norm_ffn_pre[l]), row(norm_ffn_post[l])
    y_p = _peer_call(x1_p, mod_p, npre2, npost2, wct, u_bf, vt_bf, nseq_total=n_p, seq_len=t_p)
    y_s = _peer_call(x1_s, mod_s, npre2, npost2, wct, u_bf, vt_bf, nseq_total=n_s, seq_len=t_s)

    return (y_p.reshape(n_p, t_p, D_MODEL), y_s.reshape(n_s, t_s, D_MODEL),
            conv_p[None], gla_p[None], conv_s[None], gla_s[None])
```

```python
import functools
import math

import jax
import jax.numpy as jnp
import numpy as np
from jax import lax
from jax.experimental import pallas as pl
from jax.experimental.pallas import tpu as pltpu

F32 = jnp.float32
BF16 = jnp.bfloat16

D_MODEL = 1024
CONV_DIM = 512
CONV_W = 3
GLA_HEADS = 4
GLA_DK = 512
GLA_DV = 1024
GLA_HEAD_K = 128
GLA_HEAD_V = 256
GLA_GATE_RANK = 16
GLA_GATE_TAU = 16.0
PEER_HEADS = 8
PEER_NKEYS = 128
PEER_HALF = 128
PEER_TOPK = 16
N_MOD = 6
EPS = 1e-6
W_IN_MAIN = 3 * CONV_DIM + 2 * GLA_DK + 2 * GLA_DV + 2 * D_MODEL
OFF_BG, OFF_CG, OFF_HIN = 0, 512, 1024
OFF_Q, OFF_K, OFF_V, OFF_R, OFF_GA, OFF_GB = 1536, 2048, 2560, 3584, 4608, 5632

LANES = 128
SUBLANES = 8
VMEM_LIMIT_BYTES = 58 * 1024 * 1024

GLA_CHUNK = 128
MIX_TB_PROMPT = 512
MIX_SEQ_SAMPLE = 16
PEER_TB = 512
PEER_EC = 2048
PEER_I1 = PEER_EC // PEER_NKEYS
GELU_C0 = math.sqrt(2.0 / math.pi)
GELU_C1 = 0.044715


def _dot(a, b):
    return jnp.dot(a, b, preferred_element_type=F32)


def _dot_nt(a, b):
    return lax.dot_general(a, b, (((1,), (1,)), ((), ())), preferred_element_type=F32)


def _dot_tn(a, b):
    return lax.dot_general(a, b, (((0,), (0,)), ((), ())), preferred_element_type=F32)


def _split2(x):
    p0 = x.astype(BF16)
    return p0, (x - p0.astype(F32)).astype(BF16)


def _rms(x, w):
    return x * lax.rsqrt(jnp.mean(x * x, axis=-1, keepdims=True) + EPS) * w


def _sigmoid(x):
    return 1.0 / (1.0 + jnp.exp(-x))


def _resident(shape, index_map):
    return pl.BlockSpec(shape, index_map, pipeline_mode=pl.Buffered(1))


def _rows_from_segments(seg_vals, seg_len):
    nseg, _, d = seg_vals.shape
    return jnp.broadcast_to(seg_vals, (nseg, seg_len, d)).reshape(nseg * seg_len, d)


def _ada_kernel(c_ref, w_ref, b_ref, o_ref):
    c = c_ref[...]
    a = (c * _sigmoid(c)).astype(BF16)
    o_ref[...] = _dot(a, w_ref[...].astype(BF16)) + b_ref[...]


def _ada_call(c_all, w_ada, b_ada):
    rows = c_all.shape[0]
    ncols = w_ada.shape[1]
    bn = 1536
    return pl.pallas_call(
        _ada_kernel,
        grid=(ncols // bn,),
        in_specs=[
            pl.BlockSpec((rows, D_MODEL), lambda j: (0, 0)),
            pl.BlockSpec((D_MODEL, bn), lambda j: (0, j)),
            pl.BlockSpec((1, bn), lambda j: (0, j)),
        ],
        out_specs=pl.BlockSpec((rows, bn), lambda j: (0, j)),
        out_shape=jax.ShapeDtypeStruct((rows, ncols), F32),
        compiler_params=pltpu.CompilerParams(
            dimension_semantics=("arbitrary",), vmem_limit_bytes=VMEM_LIMIT_BYTES),
        name="ada_mod",
    )(c_all, w_ada, b_ada.reshape(1, ncols))


def _gla_levels(seg_len):
    return [b for b in (1, 2, 4, 8, 16, 32, 64) if 2 * b <= seg_len]


def _gla_consts(chunk, seg_len):
    t = np.arange(chunk)
    seg = t // seg_len
    tri = (t[None, :] <= t[:, None]) & (seg[None, :] == seg[:, None])
    masks = []
    for b in _gla_levels(seg_len):
        parent = t // (2 * b)
        upper = (t % (2 * b)) >= b
        masks.append((parent[:, None] == parent[None, :]) & upper[:, None] & ~upper[None, :])
    masks.append(np.eye(chunk, dtype=bool))
    segones = np.zeros((chunk, LANES), np.float32)
    segones[t, seg] = 1.0
    return (jnp.asarray(tri.astype(np.float32), BF16),
            jnp.asarray(np.stack(masks).astype(np.float32)),
            jnp.asarray(segones, BF16))


def _group_row(x, group, row):
    n, d = x.shape
    if group >= SUBLANES:
        picked = x.reshape(n // group, group, d)[:, row:row + 1, :]
        return jnp.broadcast_to(picked, (n // group, group, d)).reshape(n, d)
    pos = lax.broadcasted_iota(jnp.int32, (n, 1), 0) % group
    out = x
    for p in range(group):
        if p != row:
            out = jnp.where(pos == p, pltpu.roll(x, (p - row) % n, 0), out)
    return out


def _gla_chunk(q, k, v, la, a_all, masks, segones, states, seg_len):
    c = q.shape[0]
    nseg = c // seg_len
    levels = _gla_levels(seg_len)
    nl = len(levels)
    pieces = _split2(la)
    cum = sum(_dot(a_all, p) for p in pieces)
    cum_last_t = sum(_dot_tn(p, segones) for p in pieces)
    decay_cols = jnp.exp(cum_last_t)
    e_cum = jnp.exp(cum)
    e_last = jnp.exp(_group_row(cum, seg_len, seg_len - 1) - cum)
    e_level = [jnp.exp(-jnp.abs(cum - _group_row(cum, 2 * b, b - 1))) for b in levels]
    outs = []
    new_states = [[None] * GLA_HEADS for _ in range(nseg)]
    for h in range(GLA_HEADS):
        ks = slice(h * GLA_HEAD_K, (h + 1) * GLA_HEAD_K)
        vs = slice(h * GLA_HEAD_V, (h + 1) * GLA_HEAD_V)
        qh, kh = q[:, ks], k[:, ks]
        vh = v[:, vs].astype(BF16)
        scores = _dot_nt(qh.astype(BF16), kh.astype(BF16)) * masks[nl]
        for li in range(nl):
            eb = e_level[li][:, ks]
            scores = scores + _dot_nt((qh * eb).astype(BF16), (kh * eb).astype(BF16)) * masks[li]
        o_h = _dot(scores.astype(BF16), vh)
        qt = qh * e_cum[:, ks]
        kt = kh * e_last[:, ks]
        vf = v[:, vs]
        inter = []
        for s in range(nseg):
            rows = slice(s * seg_len, (s + 1) * seg_len)
            st = states[s][h]
            inter.append(_dot(qt[rows].astype(BF16), st.astype(BF16)))
            new_states[s][h] = decay_cols[ks, s:s + 1] * st + _dot_tn(
                kt[rows].astype(BF16), vf[rows].astype(BF16))
        o_h = o_h + (inter[0] if nseg == 1 else jnp.concatenate(inter, axis=0))
        outs.append(o_h)
    return jnp.concatenate(outs, axis=1), new_states


def _mixer_kernel(cfg, *refs):
    tb, nseq, carried = cfg["tb"], cfg["nseq"], cfg["carried"]
    refs = list(refs)
    x_ref, mod_ref = refs[:2]
    cst_ref, gst_ref = (None, None) if carried else refs[2:4]
    (npre_ref, npost_ref, win_ref, wal_ref, convw_ref, convb_ref, walpha_ref, balpha_ref,
     gnorm_ref, wpa_ref, wpb_ref, wout_ref, aall_ref, masks_ref, segones_ref,
     xo_ref, cnew_ref, gnew_ref, proj_ref, o_ref, carry_ref, state_ref) = refs[2 if carried else 4:]
    lseq = tb // nseq
    chunk = min(GLA_CHUNK, tb)
    seg_len = min(lseq, chunk)
    step = pl.program_id(1)

    if carried:
        @pl.when(step == 0)
        def _():
            carry_ref[...] = jnp.zeros_like(carry_ref)
            state_ref[...] = jnp.zeros_like(state_ref)

    x = x_ref[...]
    mod = mod_ref[...]
    sh = _rows_from_segments(mod[:, :, 0:D_MODEL], lseq)
    sc = _rows_from_segments(mod[:, :, D_MODEL:2 * D_MODEL], lseq)
    gate = _rows_from_segments(mod[:, :, 2 * D_MODEL:3 * D_MODEL], lseq)
    h = (_rms(x, npre_ref[...]) * (1.0 + sc) + sh).astype(BF16)
    proj_ref[...] = _dot(h, win_ref[...])
    alow = _dot(h, wal_ref[...])

    u = proj_ref[:, OFF_CG:OFF_CG + CONV_DIM] * proj_ref[:, OFF_HIN:OFF_HIN + CONV_DIM]
    prev = carry_ref[...] if carried else cst_ref[...]
    prev0 = _rows_from_segments(prev[:, 0:1, :], lseq)
    prev1 = _rows_from_segments(prev[:, 1:2, :], lseq)
    pos = lax.broadcasted_iota(jnp.int32, (tb, 1), 0) % lseq
    u1 = jnp.where(pos == 0, prev1, pltpu.roll(u, 1, 0))
    u2 = jnp.where(pos == 0, prev0, jnp.where(pos == 1, prev1, pltpu.roll(u, 2, 0)))
    cw = convw_ref[...]
    y_conv = convb_ref[...] + cw[2:3, :] * u + cw[0:1, :] * u2 + cw[1:2, :] * u1
    new_buf = u.reshape(nseq, lseq, CONV_DIM)[:, lseq - 2:lseq, :]
    if carried:
        carry_ref[...] = new_buf
    cnew_ref[...] = new_buf
    branch_a = _dot((proj_ref[:, OFF_BG:OFF_BG + CONV_DIM] * y_conv).astype(BF16), wpa_ref[...])

    z = _dot(alow.astype(BF16), walpha_ref[...]) + balpha_ref[...]
    log_a = (jnp.minimum(z, 0.0) - jnp.log1p(jnp.exp(-jnp.abs(z)))) * (1.0 / GLA_GATE_TAU)
    a_all = aall_ref[...]
    masks = [masks_ref[i] for i in range(masks_ref.shape[0])]
    segones = segones_ref[...]
    nseg = chunk // seg_len
    for c in range(tb // chunk):
        rows = slice(c * chunk, (c + 1) * chunk)
        if carried:
            states = [[state_ref[hh] for hh in range(GLA_HEADS)]]
        else:
            states = [[gst_ref[c * nseg + s, hh] for hh in range(GLA_HEADS)] for s in range(nseg)]
        o_c, new_states = _gla_chunk(
            proj_ref[rows, OFF_Q:OFF_Q + GLA_DK] * (GLA_HEAD_K ** -0.5),
            proj_ref[rows, OFF_K:OFF_K + GLA_DK],
            proj_ref[rows, OFF_V:OFF_V + GLA_DV],
            log_a[rows], a_all, masks, segones, states, seg_len)
        o_ref[rows, :] = o_c
        for s in range(nseg):
            for hh in range(GLA_HEADS):
                if carried:
                    state_ref[hh] = new_states[s][hh]
                else:
                    gnew_ref[c * nseg + s, hh] = new_states[s][hh]
    if carried:
        gnew_ref[0] = state_ref[...]

    gn = gnorm_ref[...]
    normed = []
    for hh in range(GLA_HEADS):
        vs = slice(hh * GLA_HEAD_V, (hh + 1) * GLA_HEAD_V)
        normed.append(_rms(o_ref[:, vs], gn[:, vs]))
    r = proj_ref[:, OFF_R:OFF_R + GLA_DV]
    ob = jnp.concatenate(normed, axis=1) * (r * _sigmoid(r))
    branch_b = _dot(ob.astype(BF16), wpb_ref[...])

    merged = (_sigmoid(proj_ref[:, OFF_GA:OFF_GA + D_MODEL]) * branch_a
              + _sigmoid(proj_ref[:, OFF_GB:OFF_GB + D_MODEL]) * branch_b)
    m = _dot(merged.astype(BF16), wout_ref[...])
    xo_ref[...] = x + gate * _rms(m, npost_ref[...])


def _mixer_call(x2d, mod3, conv_state, gla_state, weights, *, nseq_total, seq_len, carried):
    if carried:
        tb, nseq = MIX_TB_PROMPT, 1
        grid = (nseq_total, seq_len // tb)
        x_map = lambda b, c: (b * (seq_len // tb) + c, 0)
    else:
        nseq = MIX_SEQ_SAMPLE
        tb = nseq * seq_len
        grid = (nseq_total // nseq, 1)
        x_map = lambda b, c: (b, 0)
    chunk = min(GLA_CHUNK, tb)
    seg_len = min(tb // nseq, chunk)
    a_all, masks, segones = _gla_consts(chunk, seg_len)
    cfg = dict(tb=tb, nseq=nseq, carried=carried)
    const2 = lambda b, c: (0, 0)
    const3 = lambda b, c: (0, 0, 0)
    (npre, npost, w_in_main, w_al, conv_w, conv_b, w_alpha, b_alpha, gnorm, wpa, wpb, wout) = weights
    n_tok = x2d.shape[0]
    state_shape = (nseq, GLA_HEADS, GLA_HEAD_K, GLA_HEAD_V)
    state_in_spec = pl.BlockSpec(state_shape, lambda b, c: (b, 0, 0, 0))
    state_spec = pl.BlockSpec(
        state_shape, lambda b, c: (b, 0, 0, 0),
        **({} if carried else dict(pipeline_mode=pl.Buffered(1))))
    conv_spec = pl.BlockSpec((nseq, CONV_W - 1, CONV_DIM), lambda b, c: (b, 0, 0))
    state_operands = () if carried else (conv_state, gla_state)
    in_specs = [
        pl.BlockSpec((tb, D_MODEL), x_map),
        pl.BlockSpec((nseq, 1, 3 * D_MODEL), lambda b, c: (b, 0, 0)),
        *(() if carried else (conv_spec, state_in_spec)),
        _resident((1, D_MODEL), const2),
        _resident((1, D_MODEL), const2),
        _resident((D_MODEL, W_IN_MAIN), const2),
        _resident((D_MODEL, LANES), const2),
        _resident((CONV_W, CONV_DIM), const2),
        _resident((1, CONV_DIM), const2),
        _resident((LANES, GLA_DK), const2),
        _resident((1, GLA_DK), const2),
        _resident((1, GLA_DV), const2),
        _resident((CONV_DIM, D_MODEL), const2),
        _resident((GLA_DV, D_MODEL), const2),
        _resident((D_MODEL, D_MODEL), const2),
        _resident(a_all.shape, const2),
        _resident(masks.shape, const3),
        _resident(segones.shape, const2),
    ]
    out_specs = [pl.BlockSpec((tb, D_MODEL), x_map), conv_spec, state_spec]
    out_shape = [
        jax.ShapeDtypeStruct((n_tok, D_MODEL), F32),
        jax.ShapeDtypeStruct((nseq_total, CONV_W - 1, CONV_DIM), F32),
        jax.ShapeDtypeStruct((nseq_total, GLA_HEADS, GLA_HEAD_K, GLA_HEAD_V), F32),
    ]
    scratch = [
        pltpu.VMEM((tb, W_IN_MAIN), F32),
        pltpu.VMEM((tb, GLA_DV), F32),
        pltpu.VMEM((1, CONV_W - 1, CONV_DIM), F32),
        pltpu.VMEM((GLA_HEADS, GLA_HEAD_K, GLA_HEAD_V), F32),
    ]
    return pl.pallas_call(
        functools.partial(_mixer_kernel, cfg),
        grid=grid,
        in_specs=in_specs,
        out_specs=out_specs,
        out_shape=out_shape,
        scratch_shapes=scratch,
        compiler_params=pltpu.CompilerParams(
            dimension_semantics=("arbitrary", "arbitrary"), vmem_limit_bytes=VMEM_LIMIT_BYTES),
        name="mixer_prompt" if carried else "mixer_sample",
    )(x2d, mod3, *state_operands, npre, npost, w_in_main, w_al, conv_w, conv_b, w_alpha,
      b_alpha, gnorm, wpa, wpb, wout, a_all, masks, segones)


def _wc_kernel(sk_ref, wq_ref, o_ref):
    o_ref[0] = lax.dot_general(sk_ref[0], wq_ref[...], (((1,), (1,)), ((), ())),
                               precision=lax.Precision.HIGHEST,
                               preferred_element_type=F32).astype(BF16)


def _wc_call(subkeys, wq):
    nhp = PEER_HEADS * 2
    out = pl.pallas_call(
        _wc_kernel,
        grid=(nhp,),
        in_specs=[
            pl.BlockSpec((1, PEER_NKEYS, PEER_HALF), lambda i: (i, 0, 0)),
            pl.BlockSpec((D_MODEL, PEER_HALF), lambda i: (0, i)),
        ],
        out_specs=pl.BlockSpec((1, PEER_NKEYS, D_MODEL), lambda i: (i, 0, 0)),
        out_shape=jax.ShapeDtypeStruct((nhp, PEER_NKEYS, D_MODEL), BF16),
        compiler_params=pltpu.CompilerParams(
            dimension_semantics=("arbitrary",), vmem_limit_bytes=VMEM_LIMIT_BYTES),
        name="peer_wc",
    )(subkeys.reshape(nhp, PEER_NKEYS, PEER_HALF), wq)
    return out.reshape(PEER_HEADS, 2, PEER_NKEYS, D_MODEL).transpose(1, 0, 2, 3).reshape(
        2 * PEER_HEADS * PEER_NKEYS, D_MODEL)


def _cand_pairs():
    n = PEER_TOPK + 1
    return [(a, b) for a in range(n) for b in range(n) if (a + 1) * (b + 1) <= n]


def _tree_reduce(op, xs):
    xs = list(xs)
    while len(xs) > 1:
        xs = [op(xs[i], xs[i + 1]) if i + 1 < len(xs) else xs[i] for i in range(0, len(xs), 2)]
    return xs[0]


def _desc_distinct(vals, count):
    out = [_tree_reduce(jnp.maximum, vals)]
    for _ in range(count - 1):
        m = out[-1]
        out.append(_tree_reduce(jnp.maximum, [jnp.where(v < m, v, -jnp.inf) for v in vals]))
    return out


def _store_gate_factors(st_ref, e1_ref, e2_ref, lg, h, max1, max2, half_inv_z_row):
    s1 = st_ref[lg, h * PEER_NKEYS:(h + 1) * PEER_NKEYS, :]
    row2 = (PEER_HEADS + h) * PEER_NKEYS
    s2 = st_ref[lg, row2:row2 + PEER_NKEYS, :]
    e1_ref[lg, h] = jnp.exp(s1 - max1) * half_inv_z_row
    e2_ref[lg, h] = jnp.exp(s2 - max2)


def _peer_select(st_ref, c1_ref, e1_ref, e2_ref, lg):
    nk = PEER_TOPK + 1
    tops = [[None] * PEER_HEADS for _ in range(2)]
    bad = jnp.zeros((1, LANES), F32)
    for p in range(2):
        for h in range(PEER_HEADS):
            r0 = (p * PEER_HEADS + h) * PEER_NKEYS
            s = st_ref[lg, r0:r0 + PEER_NKEYS, :]
            out = [jnp.max(s, axis=0, keepdims=True)]
            for _ in range(nk - 1):
                out.append(jnp.max(jnp.where(s < out[-1], s, -jnp.inf), axis=0, keepdims=True))
            tops[p][h] = out
            n_top = jnp.sum(jnp.where(s >= out[-1], 1.0, 0.0), axis=0, keepdims=True)
            bad = jnp.maximum(bad, jnp.where(n_top != float(nk), 1.0, 0.0))
    packed = [[jnp.concatenate([tops[p][h][k] for h in range(PEER_HEADS)], axis=0)
               for k in range(nk)] for p in range(2)]
    cands = [packed[0][a] + packed[1][b] for (a, b) in _cand_pairs()]
    t = _desc_distinct(cands, nk)
    n_top = _tree_reduce(jnp.add, [jnp.where(c >= t[-1], 1.0, 0.0) for c in cands])
    bad = jnp.maximum(bad, jnp.max(jnp.where(n_top != float(nk), 1.0, 0.0), axis=0, keepdims=True))
    tau = 0.5 * (t[PEER_TOPK - 1] + t[PEER_TOPK])
    z = jnp.ones_like(t[0])
    for k in range(1, PEER_TOPK):
        z = z + jnp.exp(t[k] - t[0])
    half_inv_z = 0.5 / z
    for h in range(PEER_HEADS):
        s1 = st_ref[lg, h * PEER_NKEYS:(h + 1) * PEER_NKEYS, :]
        c1_ref[lg, h] = tau[h:h + 1, :] - s1
        _store_gate_factors(st_ref, e1_ref, e2_ref, lg, h, tops[0][h][0], tops[1][h][0],
                            half_inv_z[h:h + 1, :])

    @pl.when(jnp.max(bad) > 0.0)
    def _():
        for h in range(PEER_HEADS):
            row2 = (PEER_HEADS + h) * PEER_NKEYS
            s2 = st_ref[lg, row2:row2 + PEER_NKEYS, :]
            c1 = c1_ref[lg, h]
            n_sel = jnp.zeros((PEER_NKEYS, LANES), F32)
            for k in range(nk):
                v = tops[1][h][k]
                mult = jnp.sum(jnp.where(s2 == v, 1.0, 0.0), axis=0, keepdims=True)
                n_sel = n_sel + jnp.where(v >= c1, mult, 0.0)
            total = jnp.sum(n_sel, axis=0, keepdims=True)

            @pl.when(jnp.max(jnp.where(total != float(PEER_TOPK), 1.0, 0.0)) > 0.0)
            def _(h=h):
                _peer_select_exact(st_ref, c1_ref, e1_ref, e2_ref, lg, h)


def _peer_select_exact(st_ref, c1_ref, e1_ref, e2_ref, lg, h):
    n_pairs = PEER_TOPK * PEER_TOPK
    row1 = h * PEER_NKEYS
    row2 = (PEER_HEADS + h) * PEER_NKEYS
    s1 = st_ref[lg, row1:row1 + PEER_NKEYS, :]
    s2 = st_ref[lg, row2:row2 + PEER_NKEYS, :]
    key_id = lax.broadcasted_iota(jnp.int32, (1, PEER_NKEYS, LANES), 1).astype(F32)
    slot_id = lax.broadcasted_iota(jnp.int32, (1, PEER_TOPK, 1), 1)

    def pick(k, state):
        rem, rank, vals = state
        m = jnp.max(rem, axis=1, keepdims=True)
        first = jnp.min(jnp.where(rem == m, key_id, float(PEER_NKEYS)), axis=1, keepdims=True)
        hit = key_id == first
        return (jnp.where(hit, -jnp.inf, rem),
                jnp.where(hit, lax.convert_element_type(k, F32), rank),
                jnp.where(slot_id == k, m, vals))
    _, rank, vals = lax.fori_loop(
        0, PEER_TOPK, pick,
        (jnp.stack([s1, s2]), jnp.full((2, PEER_NKEYS, LANES), float(PEER_TOPK), F32),
         jnp.zeros((2, PEER_TOPK, LANES), F32)))

    cand = (vals[0][:, None, :] + vals[1][None, :, :]).reshape(n_pairs, LANES)
    flat = lax.broadcasted_iota(jnp.int32, (n_pairs, 1), 0).astype(F32)

    def pick_pair(k, state):
        rem, taken, z, t0 = state
        m = jnp.max(rem, axis=0, keepdims=True)
        first = jnp.min(jnp.where(rem == m, flat, float(n_pairs)), axis=0, keepdims=True)
        hit = flat == first
        taken = taken + jnp.sum(jnp.where(hit, 1.0, 0.0).reshape(PEER_TOPK, PEER_TOPK, LANES), axis=1)
        t0 = jnp.where(k == 0, m, t0)
        return jnp.where(hit, -jnp.inf, rem), taken, z + jnp.exp(m - t0), t0
    zeros = jnp.zeros((1, LANES), F32)
    _, taken, z, _ = lax.fori_loop(0, PEER_TOPK, pick_pair,
                                   (cand, jnp.zeros((PEER_TOPK, LANES), F32), zeros, zeros))
    n1 = jnp.zeros((PEER_NKEYS, LANES), F32)
    for a in range(PEER_TOPK):
        n1 = n1 + jnp.where(rank[0] == float(a), taken[a:a + 1, :], 0.0)
    e1_ref[lg, h] = jnp.exp(s1 - vals[0][0:1, :]) * (0.5 / z)
    e2_ref[lg, h] = jnp.exp(s2 - vals[1][0:1, :])
    c1_ref[lg, h] = 0.5 - n1
    st_ref[lg, row2:row2 + PEER_NKEYS, :] = -rank[1]


def _peer_kernel(cfg, x_ref, mod_ref, npre_ref, npost_ref, wct_ref, u_ref, vt_ref, y_ref,
                 h2_ref, st_ref, c1_ref, e1_ref, e2_ref, a_ref, w_ref, acc_ref):
    tb, nseq = cfg["tb"], cfg["nseq"]
    lseq = tb // nseq
    e = pl.program_id(1)
    n_e = pl.num_programs(1)
    n_lg = tb // LANES

    def lane_group_rows(seg_vals, lg):
        if nseq == 1:
            return _rows_from_segments(seg_vals, LANES)
        per_lg = nseq // n_lg
        return _rows_from_segments(seg_vals[lg * per_lg:(lg + 1) * per_lg], lseq)

    @pl.when(e == 0)
    def _():
        mod = mod_ref[...]
        for lg in range(n_lg):
            sh = lane_group_rows(mod[:, :, 0:D_MODEL], lg)
            sc = lane_group_rows(mod[:, :, D_MODEL:2 * D_MODEL], lg)
            h2 = _rms(x_ref[lg * LANES:(lg + 1) * LANES, :], npre_ref[...]) * (1.0 + sc) + sh
            h2_ref[:, lg * LANES:(lg + 1) * LANES] = h2.T.astype(BF16)
        score_rows = 2 * PEER_NKEYS
        for r in range(0, 2 * PEER_HEADS * PEER_NKEYS, score_rows):
            st = _dot(wct_ref[r:r + score_rows, :], h2_ref[...])
            for lg in range(n_lg):
                st_ref[lg, r:r + score_rows, :] = st[:, lg * LANES:(lg + 1) * LANES]

        def select(lg, carry):
            _peer_select(st_ref, c1_ref, e1_ref, e2_ref, lg)
            return carry
        lax.fori_loop(0, n_lg, select, 0)
        acc_ref[...] = jnp.zeros_like(acc_ref)

    a_all = _dot(u_ref[...], h2_ref[...])
    for lg in range(n_lg):
        a_ref[lg] = a_all[:, lg * LANES:(lg + 1) * LANES]
    i1_tile = pl.ds(pl.multiple_of(e * PEER_I1, SUBLANES), PEER_I1)

    def gate_lane_group(lg, carry):
        lanes = pl.ds(pl.multiple_of(lg * LANES, LANES), LANES)
        c1t = [c1_ref[lg, h, i1_tile, :] for h in range(PEER_HEADS)]
        e1t = [e1_ref[lg, h, i1_tile, :] for h in range(PEER_HEADS)]
        for i1 in range(PEER_I1):
            g = jnp.zeros((PEER_NKEYS, LANES), F32)
            for h in range(PEER_HEADS):
                row2 = (PEER_HEADS + h) * PEER_NKEYS
                s2 = st_ref[lg, row2:row2 + PEER_NKEYS, :]
                p = e2_ref[lg, h] * e1t[h][i1:i1 + 1, :]
                g = g + jnp.where(s2 >= c1t[h][i1:i1 + 1, :], p, 0.0)
            rows = slice(i1 * PEER_NKEYS, (i1 + 1) * PEER_NKEYS)
            a = a_ref[lg, rows, :]
            act2 = a + a * jnp.tanh(a * (GELU_C0 + (GELU_C0 * GELU_C1) * (a * a)))
            w_ref[rows, lanes] = (act2 * g).astype(BF16)
        return carry
    lax.fori_loop(0, n_lg, gate_lane_group, 0)
    acc_ref[...] += _dot(vt_ref[...], w_ref[...])

    @pl.when(e == n_e - 1)
    def _():
        mod = mod_ref[...]
        for lg in range(n_lg):
            rows = slice(lg * LANES, (lg + 1) * LANES)
            gate = lane_group_rows(mod[:, :, 2 * D_MODEL:3 * D_MODEL], lg)
            out = acc_ref[:, rows].T
            y_ref[rows, :] = x_ref[rows, :] + gate * _rms(out, npost_ref[...])


def _peer_call(x2d, mod3, npre, npost, wct, u_bf, vt_bf, *, nseq_total, seq_len):
    n_tok = x2d.shape[0]
    tb = PEER_TB
    if seq_len >= tb:
        nseq = 1
        mod_map = lambda i, e: (i // (seq_len // tb), 0, 1)
    else:
        nseq = tb // seq_len
        mod_map = lambda i, e: (i, 0, 1)
    n_exp = u_bf.shape[0]
    assert n_exp % PEER_EC == 0 and PEER_I1 % SUBLANES == 0
    cfg = dict(tb=tb, nseq=nseq)
    const2 = lambda i, e: (0, 0)
    return pl.pallas_call(
        functools.partial(_peer_kernel, cfg),
        grid=(n_tok // tb, n_exp // PEER_EC),
        in_specs=[
            pl.BlockSpec((tb, D_MODEL), lambda i, e: (i, 0)),
            pl.BlockSpec((nseq, 1, 3 * D_MODEL), mod_map,
                         **({} if nseq == 1 else dict(pipeline_mode=pl.Buffered(1)))),
            _resident((1, D_MODEL), const2),
            _resident((1, D_MODEL), const2),
            _resident(wct.shape, const2),
            pl.BlockSpec((PEER_EC, D_MODEL), lambda i, e: (e, 0)),
            pl.BlockSpec((D_MODEL, PEER_EC), lambda i, e: (0, e)),
        ],
        out_specs=pl.BlockSpec((tb, D_MODEL), lambda i, e: (i, 0)),
        out_shape=jax.ShapeDtypeStruct((n_tok, D_MODEL), F32),
        scratch_shapes=[
            pltpu.VMEM((D_MODEL, tb), BF16),
            pltpu.VMEM((tb // LANES, 2 * PEER_HEADS * PEER_NKEYS, LANES), F32),
            pltpu.VMEM((tb // LANES, PEER_HEADS, PEER_NKEYS, LANES), F32),
            pltpu.VMEM((tb // LANES, PEER_HEADS, PEER_NKEYS, LANES), F32),
            pltpu.VMEM((tb // LANES, PEER_HEADS, PEER_NKEYS, LANES), F32),
            pltpu.VMEM((tb // LANES, PEER_EC, LANES), F32),
            pltpu.VMEM((PEER_EC, tb), BF16),
            pltpu.VMEM((D_MODEL, tb), F32),
        ],
        compiler_params=pltpu.CompilerParams(
            dimension_semantics=("arbitrary", "arbitrary"), vmem_limit_bytes=VMEM_LIMIT_BYTES),
        name="peer_prompt" if nseq == 1 else "peer_sample",
    )(x2d, mod3, npre, npost, wct, u_bf, vt_bf)


def kernel(x_prompt, x_sample, c_prompt, c_sample, state_conv, state_gla, norm_mix_pre, norm_mix_post, norm_ffn_pre, norm_ffn_post, w_ada, b_ada, w_in, conv_w, conv_b, w_alpha, b_alpha, gla_norm_w, w_proj_a, w_proj_b, w_out, peer_wq, peer_subkeys, peer_u, peer_v):
    depth = w_in.shape[0]
    assert depth == 1, "single-layer trunk"
    n_p, t_p, _ = x_prompt.shape
    n_s, t_s, _ = x_sample.shape
    l = 0

    c_all = jnp.concatenate([c_prompt, c_sample], axis=0)
    mod = _ada_call(c_all, w_ada[l], b_ada[l])
    mod_p = mod[:n_p].reshape(n_p, 1, N_MOD * D_MODEL)
    mod_s = mod[n_p:].reshape(n_s, 1, N_MOD * D_MODEL)

    w_in_main = w_in[l][:, :W_IN_MAIN].astype(BF16)
    w_al = jnp.pad(w_in[l][:, W_IN_MAIN:], ((0, 0), (0, LANES - GLA_GATE_RANK))).astype(BF16)
    w_alpha_p = jnp.pad(w_alpha[l], ((0, LANES - GLA_GATE_RANK), (0, 0))).astype(BF16)
    row = lambda a: a.reshape(1, -1)
    mixer_w = (row(norm_mix_pre[l]), row(norm_mix_post[l]), w_in_main, w_al, conv_w[l], row(conv_b[l]),
               w_alpha_p, row(b_alpha[l]), row(gla_norm_w[l]), w_proj_a[l].astype(BF16),
               w_proj_b[l].astype(BF16), w_out[l].astype(BF16))

    x1_p, conv_p, gla_p = _mixer_call(
        x_prompt.reshape(n_p * t_p, D_MODEL), mod_p, None, None, mixer_w,
        nseq_total=n_p, seq_len=t_p, carried=True)
    x1_s, conv_s, gla_s = _mixer_call(
        x_sample.reshape(n_s * t_s, D_MODEL), mod_s, state_conv[l], state_gla[l], mixer_w,
        nseq_total=n_s, seq_len=t_s, carried=False)

    wct = _wc_call(peer_subkeys[l], peer_wq[l])
    u_bf = peer_u[l].astype(BF16)
    vt_bf = peer_v[l].astype(BF16).T
    npre2, npost2 = row(norm_ffn_pre[l]), row(norm_ffn_post[l])
    y_p = _peer_call(x1_p, mod_p, npre2, npost2, wct, u_bf, vt_bf, nseq_total=n_p, seq_len=t_p)
    y_s = _peer_call(x1_s, mod_s, npre2, npost2, wct, u_bf, vt_bf, nseq_total=n_s, seq_len=t_s)

    return (y_p.reshape(n_p, t_p, D_MODEL), y_s.reshape(n_s, t_s, D_MODEL),
            conv_p[None], gla_p[None], conv_s[None], gla_s[None])
```
